```python
import math
import jax, jax.numpy as jnp
from jax import lax
import numpy as np

D_MODEL = 1024
BATCH = 16
SEQ = 2048
DEPTH = 4

CHUNK = 64
Q_BLOCK = 128
N_MEM = 256
EPS = 1e-6

A_HEADS = 4
A_QK_DIM = 64
A_V_DIM = 2 * A_QK_DIM
A_WIDTH = A_HEADS * A_V_DIM
REL_BUCKETS = 32
REL_MAX_DIST = 128

POOL_WINDOWS = (2, 4, 8, 16)
POOL_GROUPS = len(POOL_WINDOWS)
POOL_GROUP_DIM = 128
POOL_WIDTH = POOL_GROUPS * POOL_GROUP_DIM

SSD_HEADS = 8
SSD_HEAD_DIM = 64
SSD_WIDTH = SSD_HEADS * SSD_HEAD_DIM
SSD_GROUPS = 2
SSD_STATE = 64
SSD_CONV = 4
SSD_CONV_CH = SSD_WIDTH + 2 * SSD_GROUPS * SSD_STATE

N_BRANCH = 3

COL_Q = 0
COL_K = COL_Q + A_HEADS * 2 * A_QK_DIM
COL_V = COL_K + A_HEADS * 2 * A_QK_DIM
COL_POOL = COL_V + A_WIDTH
COL_Z = COL_POOL + POOL_WIDTH
COL_XBC = COL_Z + SSD_WIDTH
COL_DT = COL_XBC + SSD_CONV_CH
COL_GATE = COL_DT + SSD_HEADS
IN_COLS = COL_GATE + N_BRANCH * D_MODEL

X_HEADS = 4
X_HEAD_DIM = 64
X_WIDTH = X_HEADS * X_HEAD_DIM

N_EGROUPS = 4
EXPERTS_PER_GROUP = 4
N_EXPERTS = N_EGROUPS * EXPERTS_PER_GROUP
EXPERT_TOPK = 2
EXPERT_FF = 512

kernel_name = 'hybrid_diffattn_pool_ssd_hmoe_trunk'


def rmsnorm(x, g):
    xf = x.astype(jnp.float32)
    y = xf * lax.rsqrt(jnp.mean(xf * xf, axis=-1, keepdims=True) + EPS)
    return (y * g.astype(jnp.float32)).astype(x.dtype)


def rel_bucket(rel):
    nb = REL_BUCKETS // 2
    max_exact = nb // 2
    side = jnp.where(rel > 0, nb, 0)
    n = jnp.abs(rel)
    n_f = jnp.maximum(n, 1).astype(jnp.float32)
    large = max_exact + (jnp.log(n_f / max_exact) / math.log(REL_MAX_DIST / max_exact)
                         * (nb - max_exact)).astype(jnp.int32)
    large = jnp.minimum(large, nb - 1)
    return side + jnp.where(n < max_exact, n, large)


def diff_attention(q, k, v, lam, rel_table, subln_g, lam_init):
    B, S = q.shape[0], q.shape[1]
    nb = S // Q_BLOCK
    qb = (q * (A_QK_DIM ** -0.5)).reshape(B, nb, Q_BLOCK, A_HEADS, 2, A_QK_DIM)
    qb = jnp.moveaxis(qb, 1, 0)
    kpos = jnp.arange(S)

    def block(args):
        qblk, i = args
        qpos = i * Q_BLOCK + jnp.arange(Q_BLOCK)
        bias = rel_table[rel_bucket(kpos[None, :] - qpos[:, None])]
        bias = jnp.transpose(bias, (2, 0, 1)).astype(jnp.float32)
        visible = (kpos[None, :] // CHUNK) <= (qpos[:, None] // CHUNK)
        s = jnp.einsum('bqhmd,bkhmd->bhmqk', qblk, k).astype(jnp.float32)
        s = s + bias[None, :, None]
        s = jnp.where(visible, s, -jnp.inf)
        p = jax.nn.softmax(s, axis=-1)
        attn = p[:, :, 0] - lam * p[:, :, 1]
        return jnp.einsum('bhqk,bkhv->bqhv', attn.astype(v.dtype), v)

    o = lax.map(block, (qb, jnp.arange(nb)))
    o = jnp.moveaxis(o, 0, 1).reshape(B, S, A_HEADS, A_V_DIM)
    o = rmsnorm(o, subln_g) * (1.0 - lam_init)
    return o.reshape(B, S, A_WIDTH)


def pool_mixer(u, w_group, scale):
    B, S = u.shape[0], u.shape[1]
    G = POOL_GROUP_DIM
    uf = u.astype(jnp.float32)
    c = jnp.cumsum(uf, axis=1)
    t = jnp.arange(1, S + 1, dtype=jnp.float32)
    outs = []
    for gi, w in enumerate(POOL_WINDOWS):
        cg = c[..., gi * G:(gi + 1) * G]
        shifted = jnp.pad(cg, ((0, 0), (w, 0), (0, 0)))[:, :S]
        mean = (cg - shifted) / jnp.minimum(t, float(w))[None, :, None]
        outs.append(mean - uf[..., gi * G:(gi + 1) * G])
    d = jnp.stack(outs, axis=2).astype(u.dtype)
    y = jnp.einsum('bsgc,gcd->bsgd', d, w_group)
    return y.reshape(B, S, POOL_WIDTH) * scale


def ssd_chunked(xs, dt, A, Bm, Cm):
    Bsz, S, H, P = xs.shape
    N = Bm.shape[-1]
    nc = S // CHUNK
    x_ = xs.reshape(Bsz, nc, CHUNK, H, P)
    dt_ = dt.reshape(Bsz, nc, CHUNK, H)
    B_ = Bm.reshape(Bsz, nc, CHUNK, H, N)
    C_ = Cm.reshape(Bsz, nc, CHUNK, H, N)
    dA_cs = jnp.cumsum(dt_ * A, axis=2)
    seg = dA_cs[:, :, :, None, :] - dA_cs[:, :, None, :, :]
    causal = jnp.tril(jnp.ones((CHUNK, CHUNK), dtype=bool))[None, None, :, :, None]
    Lmat = jnp.exp(jnp.where(causal, seg, -jnp.inf))
    xdt = x_ * dt_[..., None]
    scores = jnp.einsum('bclhn,bcshn->bclsh', C_, B_)
    y_diag = jnp.einsum('bclsh,bcshp->bclhp', scores * Lmat, xdt)
    decay_to_end = jnp.exp(dA_cs[:, :, -1:, :] - dA_cs)
    states = jnp.einsum('bclhn,bclh,bclhp->bchpn', B_, decay_to_end, xdt)
    chunk_decay = jnp.exp(dA_cs[:, :, -1, :])

    def step(h, inp):
        st, dec = inp
        return h * dec[..., None, None] + st, h

    h0 = jnp.zeros((Bsz, H, P, N), dtype=xs.dtype)
    _, h_in = lax.scan(step, h0, (jnp.moveaxis(states, 1, 0), jnp.moveaxis(chunk_decay, 1, 0)))
    h_in = jnp.moveaxis(h_in, 0, 1)
    y_off = jnp.einsum('bclhn,bchpn,bclh->bclhp', C_, h_in, jnp.exp(dA_cs))
    return (y_diag + y_off).reshape(Bsz, S, H, P)


def ssd_mixer(z, xbc, dt_raw, conv_w, conv_b, dt_bias, A_log, D_skip, norm_g):
    B, S = z.shape[0], z.shape[1]
    xbc = lax.conv_general_dilated(xbc, conv_w[:, None, :], window_strides=(1,),
                                   padding=[(SSD_CONV - 1, 0)],
                                   dimension_numbers=('NWC', 'WIO', 'NWC'),
                                   feature_group_count=SSD_CONV_CH) + conv_b
    xbc = jax.nn.silu(xbc)
    GN = SSD_GROUPS * SSD_STATE
    xs = xbc[..., :SSD_WIDTH].reshape(B, S, SSD_HEADS, SSD_HEAD_DIM).astype(jnp.float32)
    Bm = xbc[..., SSD_WIDTH:SSD_WIDTH + GN].reshape(B, S, SSD_GROUPS, SSD_STATE)
    Cm = xbc[..., SSD_WIDTH + GN:].reshape(B, S, SSD_GROUPS, SSD_STATE)
    rep = SSD_HEADS // SSD_GROUPS
    Bm = jnp.repeat(Bm, rep, axis=2).astype(jnp.float32)
    Cm = jnp.repeat(Cm, rep, axis=2).astype(jnp.float32)
    dt = jax.nn.softplus(dt_raw.astype(jnp.float32) + dt_bias.astype(jnp.float32))
    A = -jnp.exp(A_log.astype(jnp.float32))
    y = ssd_chunked(xs, dt, A, Bm, Cm)
    y = y + D_skip.astype(jnp.float32)[:, None] * xs
    y = y.reshape(B, S, SSD_WIDTH) * jax.nn.silu(z.astype(jnp.float32))
    return rmsnorm(y.astype(z.dtype), norm_g)


def mem_cross_attention(h, mem_n, w_q, w_kv, w_o):
    B, S = h.shape[0], h.shape[1]
    M = mem_n.shape[1]
    q = (h @ w_q).reshape(B, S, X_HEADS, X_HEAD_DIM) * (X_HEAD_DIM ** -0.5)
    kv = mem_n @ w_kv
    k = kv[..., :X_WIDTH].reshape(B, M, X_HEADS, X_HEAD_DIM)
    v = kv[..., X_WIDTH:].reshape(B, M, X_HEADS, X_HEAD_DIM)
    s = jnp.einsum('bshd,bmhd->bhsm', q, k).astype(jnp.float32)
    p = jax.nn.softmax(s, axis=-1).astype(v.dtype)
    o = jnp.einsum('bhsm,bmhd->bshd', p, v).reshape(B, S, X_WIDTH)
    return o @ w_o


def hierarchical_moe(h, w_rg, b_rg, w_re, b_re, w_gate, w_up, w_down):
    B, S, D = h.shape
    hf = h.reshape(B * S, D)
    g_prob = jax.nn.softmax((hf @ w_rg).astype(jnp.float32) + b_rg.astype(jnp.float32), axis=-1)
    g_p, g_idx = lax.top_k(g_prob, 1)
    e_logits = ((hf @ w_re).astype(jnp.float32) + b_re.astype(jnp.float32))
    e_logits = e_logits.reshape(-1, N_EGROUPS, EXPERTS_PER_GROUP)
    g_onehot = jax.nn.one_hot(g_idx[:, 0], N_EGROUPS, dtype=jnp.float32)
    e_in = jnp.sum(e_logits * g_onehot[:, :, None], axis=1)
    e_val, e_idx = lax.top_k(e_in, EXPERT_TOPK)
    e_w = jax.nn.softmax(e_val, axis=-1)
    expert_id = g_idx * EXPERTS_PER_GROUP + e_idx
    comb = g_p * e_w
    comb_dense = jnp.sum(jax.nn.one_hot(expert_id, N_EXPERTS, dtype=jnp.float32) * comb[..., None], axis=1)
    comb_dense = comb_dense.astype(h.dtype)
    y = jnp.zeros_like(hf)
    for e in range(N_EXPERTS):
        a = jax.nn.silu(hf @ w_gate[e]) * (hf @ w_up[e])
        y = y + comb_dense[:, e:e + 1] * (a @ w_down[e])
    return y.reshape(B, S, D)


def setup_inputs(seed: int = 0) -> dict:
    key = jax.random.key(seed)
    ks = iter(jax.random.split(key, 48))
    f32 = jnp.float32
    nrm = lambda shape, s: jax.random.normal(next(ks), shape, f32) * s
    gain = lambda shape: 1.0 + 0.02 * jax.random.normal(next(ks), shape, f32)
    Dm = D_MODEL
    dt0 = jnp.exp(jax.random.uniform(next(ks), (DEPTH, SSD_HEADS), f32,
                                     math.log(1e-3), math.log(1e-1)))
    dt_bias = dt0 + jnp.log(-jnp.expm1(-dt0))
    A_log = jnp.log(jax.random.uniform(next(ks), (DEPTH, SSD_HEADS), f32, 1.0, 16.0))
    return {
        'x': nrm((BATCH, SEQ, Dm), 1.0),
        'mem': nrm((BATCH, N_MEM, Dm), 1.0),
        'w_in': nrm((DEPTH, Dm, IN_COLS), Dm ** -0.5),
        'b_gate': nrm((DEPTH, N_BRANCH * Dm), 0.02),
        'lam_q1': nrm((DEPTH, A_QK_DIM), 0.1),
        'lam_k1': nrm((DEPTH, A_QK_DIM), 0.1),
        'lam_q2': nrm((DEPTH, A_QK_DIM), 0.1),
        'lam_k2': nrm((DEPTH, A_QK_DIM), 0.1),
        'attn_subln_g': gain((DEPTH, A_V_DIM)),
        'rel_bias_table': nrm((REL_BUCKETS, A_HEADS), 0.2),
        'w_pool_group': nrm((DEPTH, POOL_GROUPS, POOL_GROUP_DIM, POOL_GROUP_DIM), POOL_GROUP_DIM ** -0.5),
        'pool_scale': 1.0 + 0.1 * jax.random.normal(next(ks), (DEPTH, POOL_WIDTH), f32),
        'ssd_conv_w': nrm((DEPTH, SSD_CONV, SSD_CONV_CH), SSD_CONV ** -0.5),
        'ssd_conv_b': nrm((DEPTH, SSD_CONV_CH), 0.02),
        'ssd_dt_bias': dt_bias,
        'ssd_A_log': A_log,
        'ssd_D': gain((DEPTH, SSD_HEADS)),
        'ssd_norm_g': gain((DEPTH, SSD_WIDTH)),
        'w_branch': nrm((DEPTH, N_BRANCH, A_WIDTH, Dm), A_WIDTH ** -0.5),
        'w_mix_out': nrm((DEPTH, Dm, Dm), Dm ** -0.5),
        'g_mix': gain((DEPTH, Dm)),
        'g_xattn': gain((DEPTH, Dm)),
        'g_mem': gain((DEPTH, Dm)),
        'w_xq': nrm((DEPTH, Dm, X_WIDTH), Dm ** -0.5),
        'w_xkv': nrm((DEPTH, Dm, 2 * X_WIDTH), Dm ** -0.5),
        'w_xo': nrm((DEPTH, X_WIDTH, Dm), X_WIDTH ** -0.5),
        'g_moe': gain((DEPTH, Dm)),
        'w_router_group': nrm((DEPTH, Dm, N_EGROUPS), Dm ** -0.5),
        'b_router_group': nrm((DEPTH, N_EGROUPS), 0.01),
        'w_router_expert': nrm((DEPTH, Dm, N_EXPERTS), Dm ** -0.5),
        'b_router_expert': nrm((DEPTH, N_EXPERTS), 0.01),
        'w_exp_gate': nrm((DEPTH, N_EXPERTS, Dm, EXPERT_FF), Dm ** -0.5),
        'w_exp_up': nrm((DEPTH, N_EXPERTS, Dm, EXPERT_FF), Dm ** -0.5),
        'w_exp_down': nrm((DEPTH, N_EXPERTS, EXPERT_FF, Dm), EXPERT_FF ** -0.5),
        'g_final': gain((Dm,)),
    }


def reference(x, mem, w_in, b_gate, lam_q1, lam_k1, lam_q2, lam_k2, attn_subln_g,
              rel_bias_table, w_pool_group, pool_scale, ssd_conv_w, ssd_conv_b,
              ssd_dt_bias, ssd_A_log, ssd_D, ssd_norm_g, w_branch, w_mix_out,
              g_mix, g_xattn, g_mem, w_xq, w_xkv, w_xo, g_moe,
              w_router_group, b_router_group, w_router_expert, b_router_expert,
              w_exp_gate, w_exp_up, w_exp_down, g_final):
    B, S, Dm = x.shape
    for l in range(DEPTH):
        lam_init = 0.8 - 0.6 * math.exp(-0.3 * l)
        h = rmsnorm(x, g_mix[l])
        proj = h @ w_in[l]
        q = proj[..., COL_Q:COL_K].reshape(B, S, A_HEADS, 2, A_QK_DIM)
        k = proj[..., COL_K:COL_V].reshape(B, S, A_HEADS, 2, A_QK_DIM)
        v = proj[..., COL_V:COL_POOL].reshape(B, S, A_HEADS, A_V_DIM)
        lam = (jnp.exp(jnp.sum(lam_q1[l].astype(jnp.float32) * lam_k1[l].astype(jnp.float32)))
               - jnp.exp(jnp.sum(lam_q2[l].astype(jnp.float32) * lam_k2[l].astype(jnp.float32)))
               + lam_init)
        o_a = diff_attention(q, k, v, lam, rel_bias_table, attn_subln_g[l], lam_init)
        o_b = pool_mixer(proj[..., COL_POOL:COL_Z], w_pool_group[l], pool_scale[l])
        o_c = ssd_mixer(proj[..., COL_Z:COL_XBC], proj[..., COL_XBC:COL_DT],
                        proj[..., COL_DT:COL_GATE], ssd_conv_w[l], ssd_conv_b[l],
                        ssd_dt_bias[l], ssd_A_log[l], ssd_D[l], ssd_norm_g[l])
        gates = jax.nn.sigmoid(proj[..., COL_GATE:] + b_gate[l]).reshape(B, S, N_BRANCH, Dm)
        merged = (gates[:, :, 0] * (o_a @ w_branch[l, 0])
                  + gates[:, :, 1] * (o_b @ w_branch[l, 1])
                  + gates[:, :, 2] * (o_c @ w_branch[l, 2]))
        x = x + merged @ w_mix_out[l]
        mem_n = rmsnorm(mem, g_mem[l])
        x = x + mem_cross_attention(rmsnorm(x, g_xattn[l]), mem_n, w_xq[l], w_xkv[l], w_xo[l])
        x = x + hierarchical_moe(rmsnorm(x, g_moe[l]), w_router_group[l], b_router_group[l],
                                 w_router_expert[l], b_router_expert[l],
                                 w_exp_gate[l], w_exp_up[l], w_exp_down[l])
    return rmsnorm(x, g_final)
```

```python
import functools
import math

import jax
import jax.numpy as jnp
from jax import lax
from jax.experimental import pallas as pl
from jax.experimental.pallas import tpu as pltpu

F32 = jnp.float32
BF16 = jnp.bfloat16

EPS = 1e-6
CHUNK = 64
N_MEM = 256

A_HEADS = 4
A_QK_DIM = 64
A_V_DIM = 128
A_WIDTH = 512
REL_BUCKETS = 32
REL_MAX_DIST = 128

POOL_WINDOWS = (2, 4, 8, 16)
POOL_GROUP_DIM = 128
POOL_WIDTH = 512

SSD_HEADS = 8
SSD_HEAD_DIM = 64
SSD_WIDTH = 512
SSD_GROUPS = 2
SSD_STATE = 64
SSD_CONV = 4
SSD_CONV_CH = 768

N_BRANCH = 3
X_HEADS = 4
X_HEAD_DIM = 64
X_WIDTH = 256

N_EGROUPS = 4
EXPERTS_PER_GROUP = 4
N_EXPERTS = 16
EXPERT_FF = 512

LANES = 128
NEG_BIG = -1e30

C_Q, C_K, C_V, C_POOL, C_Z, C_XBC, C_GATE = 0, 512, 1024, 1536, 2048, 2560, 3328
C_DTX = 6400
C_DT8 = 6912
C_END = 7040

TM_PROJ = 512
TM_MERGE = 512
TM_MOE = 512
TA = 256
VMEM_LIMIT = 56 * 1024 * 1024


def _cparams(*sem):
    return pltpu.CompilerParams(dimension_semantics=sem, vmem_limit_bytes=VMEM_LIMIT)


def _rms(x, g):
    return x * lax.rsqrt(jnp.mean(x * x, axis=-1, keepdims=True) + EPS) * g


def _dot(a, b):
    return jnp.dot(a, b, preferred_element_type=F32)


def _dot_nt(a, b):
    return lax.dot_general(a, b, (((1,), (1,)), ((), ())), preferred_element_type=F32)


def _dot_tn(a, b):
    return lax.dot_general(a, b, (((0,), (0,)), ((), ())), preferred_element_type=F32)


def _sigmoid(x):
    return 1.0 / (1.0 + jnp.exp(-x))


def _silu(x):
    return x * _sigmoid(x)


def _softplus(x):
    return jnp.maximum(x, 0.0) + jnp.log(1.0 + jnp.exp(-jnp.abs(x)))


def _shift_rows(v, s, row):
    return jnp.where(row >= s, pltpu.roll(v, s, axis=0), 0.0)


def _chunk_cumsum(v, row_in_chunk):
    s = 1
    while s < CHUNK:
        v = v + jnp.where(row_in_chunk >= s, pltpu.roll(v, s, axis=0), 0.0)
        s *= 2
    return v


def _setup_kernel(table_ref, bucket_ref, vis_ref, lq1_ref, lk1_ref, lq2_ref, lk2_ref,
                  bias_ref, lam_ref, *, lam_inits):
    for h in range(A_HEADS):
        tiles = []
        for d in range(3):
            bkt = bucket_ref[d]
            acc = jnp.zeros(bkt.shape, F32)
            for b in range(REL_BUCKETS):
                acc = acc + jnp.where(bkt == b, table_ref[b, h], 0.0)
            tiles.append(acc)
        bias_ref[h, 0] = jnp.where(vis_ref[...] > 0, tiles[0] - tiles[2], NEG_BIG)
        bias_ref[h, 1] = tiles[1] - tiles[2]
    s1 = jnp.sum(lq1_ref[...] * lk1_ref[...], axis=-1, keepdims=True)
    s2 = jnp.sum(lq2_ref[...] * lk2_ref[...], axis=-1, keepdims=True)
    row = lax.broadcasted_iota(jnp.int32, s1.shape, 0)
    init = jnp.zeros(s1.shape, F32)
    for l, v in enumerate(lam_inits):
        init = jnp.where(row == l, v, init)
    lam_ref[...] = jnp.broadcast_to(jnp.exp(s1) - jnp.exp(s2) + init, lam_ref.shape)


def _rel_bucket_index(rel):
    nb = REL_BUCKETS // 2
    max_exact = nb // 2
    side = jnp.where(rel > 0, nb, 0)
    n = jnp.abs(rel)
    n_f = jnp.maximum(n, 1).astype(F32)
    large = max_exact + (jnp.log(n_f / max_exact) / math.log(REL_MAX_DIST / max_exact)
                         * (nb - max_exact)).astype(jnp.int32)
    large = jnp.minimum(large, nb - 1)
    return side + jnp.where(n < max_exact, n, large)


def _setup(rel_bias_table, lam_q1, lam_k1, lam_q2, lam_k2, lam_inits):
    depth = lam_q1.shape[0]
    r = jnp.arange(TA)
    rel0 = r[None, :] - r[:, None]
    bucket = jnp.stack([_rel_bucket_index(rel0 - d * TA) for d in range(3)]).astype(jnp.int32)
    vis = ((r[None, :] // CHUNK) <= (r[:, None] // CHUNK)).astype(jnp.int32)
    vm = pl.BlockSpec(memory_space=pltpu.VMEM)
    return pl.pallas_call(
        functools.partial(_setup_kernel, lam_inits=lam_inits),
        out_shape=(jax.ShapeDtypeStruct((A_HEADS, 2, TA, TA), F32),
                   jax.ShapeDtypeStruct((depth, LANES), F32)),
        in_specs=[pl.BlockSpec(memory_space=pltpu.SMEM), vm, vm, vm, vm, vm, vm],
        out_specs=(vm, vm),
        name="setup_bias_lam",
    )(rel_bias_table, bucket, vis, lam_q1, lam_k1, lam_q2, lam_k2)


def _inproj_kernel(x_ref, g_ref, w_ref, bg_ref, dtbx_ref, dtb8_ref, alog8_ref,
                   qkv_ref, pool_ref, z_ref, xbc_ref, gate_ref, dtx_ref, cs8_ref):
    h = _rms(x_ref[...], g_ref[...]).astype(BF16)

    def mm(lo, hi):
        return _dot(h, w_ref[:, lo:hi])

    qkv_ref[:, 0:512] = (mm(C_Q, C_K) * (A_QK_DIM ** -0.5)).astype(BF16)
    qkv_ref[:, 512:1024] = mm(C_K, C_V).astype(BF16)
    qkv_ref[:, 1024:1536] = mm(C_V, C_POOL).astype(BF16)
    pool_ref[...] = mm(C_POOL, C_Z).astype(BF16)
    z_ref[...] = mm(C_Z, C_XBC).astype(BF16)
    xbc_ref[:, 0:512] = mm(C_XBC, C_XBC + 512).astype(BF16)
    xbc_ref[:, 512:768] = mm(C_XBC + 512, C_GATE).astype(BF16)
    for j in range(6):
        lo = C_GATE + 512 * j
        gate_ref[:, 512 * j:512 * (j + 1)] = _sigmoid(
            mm(lo, lo + 512) + bg_ref[:, 512 * j:512 * (j + 1)]).astype(BF16)
    dtx_ref[...] = _softplus(mm(C_DTX, C_DT8) + dtbx_ref[...])
    dt8 = _softplus(mm(C_DT8, C_END) + dtb8_ref[...])
    row = lax.broadcasted_iota(jnp.int32, dt8.shape, 0) % CHUNK
    cs8_ref[...] = _chunk_cumsum(dt8 * (-jnp.exp(alog8_ref[...])), row)


def _inproj(x, g, w, bg, dtbx, dtb8, alog8):
    T, D = x.shape
    tm = TM_PROJ
    const = lambda shape: pl.BlockSpec(shape, lambda i: (0, 0), pipeline_mode=pl.Buffered(1))
    row = lambda n: pl.BlockSpec((tm, n), lambda i: (i, 0))
    return pl.pallas_call(
        _inproj_kernel,
        grid=(T // tm,),
        in_specs=[row(D), const((1, D)), const((D, C_END)), const((1, 3072)),
                  const((1, 512)), const((1, LANES)), const((1, LANES))],
        out_specs=(row(1536), row(512), row(512), row(768), row(3072), row(512), row(LANES)),
        out_shape=(jax.ShapeDtypeStruct((T, 1536), BF16), jax.ShapeDtypeStruct((T, 512), BF16),
                   jax.ShapeDtypeStruct((T, 512), BF16), jax.ShapeDtypeStruct((T, 768), BF16),
                   jax.ShapeDtypeStruct((T, 3072), BF16), jax.ShapeDtypeStruct((T, 512), F32),
                   jax.ShapeDtypeStruct((T, LANES), F32)),
        compiler_params=_cparams("parallel"),
        name="inproj",
    )(x, g, w, bg, dtbx, dtb8, alog8)


def _attn_kernel(q_ref, k_ref, v_ref, bias_ref, lam_ref, g_ref, o_ref,
                 m1_ref, l1_ref, a1_ref, m2_ref, l2_ref, a2_ref, *, out_scale):
    i = pl.program_id(2)
    q = q_ref[...]
    lane = lax.broadcasted_iota(jnp.int32, q.shape, 1)
    q1 = jnp.where(lane < A_QK_DIM, q, jnp.zeros_like(q))
    q2 = jnp.where(lane >= A_QK_DIM, q, jnp.zeros_like(q))

    m1_ref[...] = jnp.full(m1_ref.shape, NEG_BIG, F32)
    m2_ref[...] = jnp.full(m2_ref.shape, NEG_BIG, F32)
    l1_ref[...] = jnp.zeros(l1_ref.shape, F32)
    l2_ref[...] = jnp.zeros(l2_ref.shape, F32)
    a1_ref[...] = jnp.zeros(a1_ref.shape, F32)
    a2_ref[...] = jnp.zeros(a2_ref.shape, F32)

    def stream(qm, kt, vt, bias, m_ref, l_ref, a_ref):
        s = _dot_nt(qm, kt)
        if bias is not None:
            s = s + bias
        m_old = m_ref[...]
        m_new = jnp.maximum(m_old, jnp.max(s, axis=-1, keepdims=True))
        alpha = jnp.exp(m_old - m_new)
        p = jnp.exp(s - m_new)
        l_ref[...] = alpha * l_ref[...] + jnp.sum(p, axis=-1, keepdims=True)
        a_ref[...] = alpha * a_ref[...] + _dot(p.astype(BF16), vt)
        m_ref[...] = m_new

    def tile(j, bias):
        start = pl.multiple_of(j * TA, TA)
        kt = k_ref[pl.ds(start, TA), :]
        vt = v_ref[pl.ds(start, TA), :]
        stream(q1, kt, vt, bias, m1_ref, l1_ref, a1_ref)
        stream(q2, kt, vt, bias, m2_ref, l2_ref, a2_ref)

    def far_body(j, c):
        tile(j, None)
        return c

    lax.fori_loop(0, jnp.maximum(i - 1, 0), far_body, 0)

    @pl.when(i >= 1)
    def _():
        tile(i - 1, bias_ref[1])

    tile(i, bias_ref[0])

    lam = lam_ref[...]
    o = a1_ref[...] / l1_ref[...] - lam * (a2_ref[...] / l2_ref[...])
    o_ref[...] = (_rms(o, g_ref[...]) * out_scale).astype(o_ref.dtype)


def _attention(qkv, bias, lam_row, subln_g, out_scale, B, S):
    qkv3 = qkv.reshape(B, S, 3 * A_WIDTH)
    nq = S // TA
    col = lambda: pl.BlockSpec((TA, 1), lambda b, h, i: (0, 0))
    return pl.pallas_call(
        functools.partial(_attn_kernel, out_scale=out_scale),
        grid=(B, A_HEADS, nq),
        in_specs=[pl.BlockSpec((None, TA, LANES), lambda b, h, i: (b, i, h)),
                  pl.BlockSpec((None, S, LANES), lambda b, h, i: (b, 0, A_HEADS + h)),
                  pl.BlockSpec((None, S, LANES), lambda b, h, i: (b, 0, 2 * A_HEADS + h)),
                  pl.BlockSpec((None, 2, TA, TA), lambda b, h, i: (h, 0, 0, 0)),
                  pl.BlockSpec((1, LANES), lambda b, h, i: (0, 0)),
                  pl.BlockSpec((1, LANES), lambda b, h, i: (0, 0))],
        out_specs=pl.BlockSpec((None, TA, LANES), lambda b, h, i: (b, i, h)),
        out_shape=jax.ShapeDtypeStruct((B, S, A_WIDTH), BF16),
        scratch_shapes=[pltpu.VMEM((TA, 1), F32), pltpu.VMEM((TA, 1), F32),
                        pltpu.VMEM((TA, A_V_DIM), F32),
                        pltpu.VMEM((TA, 1), F32), pltpu.VMEM((TA, 1), F32),
                        pltpu.VMEM((TA, A_V_DIM), F32)],
        compiler_params=_cparams("parallel", "parallel", "arbitrary"),
        name="diff_attention",
    )(qkv3, qkv3, qkv3, bias, lam_row, subln_g)


def _pool_kernel(u_ref, w_ref, scale_ref, o_ref):
    S = u_ref.shape[0]
    row = lax.broadcasted_iota(jnp.int32, (S, POOL_GROUP_DIM), 0)
    t = (row + 1).astype(F32)
    for gi, win in enumerate(POOL_WINDOWS):
        sl = slice(gi * POOL_GROUP_DIM, (gi + 1) * POOL_GROUP_DIM)
        u = u_ref[:, sl].astype(F32)
        acc = u
        s = 1
        while s < win:
            acc = acc + _shift_rows(acc, s, row)
            s *= 2
        d = acc / jnp.minimum(t, float(win)) - u
        y = _dot(d.astype(BF16), w_ref[gi])
        o_ref[:, sl] = (y * scale_ref[:, sl]).astype(o_ref.dtype)


def _pool(u, w_group, scale, B, S):
    u3 = u.reshape(B, S, POOL_WIDTH)
    return pl.pallas_call(
        _pool_kernel,
        grid=(B,),
        in_specs=[pl.BlockSpec((None, S, POOL_WIDTH), lambda b: (b, 0, 0)),
                  pl.BlockSpec((len(POOL_WINDOWS), POOL_GROUP_DIM, POOL_GROUP_DIM),
                               lambda b: (0, 0, 0)),
                  pl.BlockSpec((1, POOL_WIDTH), lambda b: (0, 0))],
        out_specs=pl.BlockSpec((None, S, POOL_WIDTH), lambda b: (b, 0, 0)),
        out_shape=jax.ShapeDtypeStruct((B, S, POOL_WIDTH), BF16),
        compiler_params=_cparams("parallel"),
        name="pool_mixer",
    )(u3, w_group, scale)


def _ssd_kernel(xbc_ref, z_ref, dtx_ref, r_ref, convw_ref, convb_ref, alogx_ref, dx_ref, g_ref,
                o_ref, conv_ref, csx_ref, state_ref):
    S = xbc_ref.shape[0]
    nc = S // CHUNK
    GN = SSD_GROUPS * SSD_STATE
    HP = SSD_WIDTH

    xin = xbc_ref[...].astype(F32)
    row = lax.broadcasted_iota(jnp.int32, xin.shape, 0)
    acc = xin * convw_ref[SSD_CONV - 1:SSD_CONV, :] + convb_ref[...]
    for s in range(1, SSD_CONV):
        acc = acc + _shift_rows(xin, s, row) * convw_ref[SSD_CONV - 1 - s:SSD_CONV - s, :]
    conv_ref[...] = _silu(acc)

    row_c = lax.broadcasted_iota(jnp.int32, (S, HP), 0) % CHUNK
    csx_ref[...] = _chunk_cumsum(dtx_ref[...] * (-jnp.exp(alogx_ref[...])), row_c)

    state_ref[...] = jnp.zeros(state_ref.shape, F32)

    li = lax.broadcasted_iota(jnp.int32, (CHUNK, HP), 0)
    ci = lax.broadcasted_iota(jnp.int32, (CHUNK, HP), 1)
    tril = li >= (ci % CHUNK)
    br = lax.broadcasted_iota(jnp.int32, (HP, GN), 0)
    bc = lax.broadcasted_iota(jnp.int32, (HP, GN), 1)
    bmask = (br // (HP // SSD_GROUPS)) == (bc // SSD_STATE)
    sr = lax.broadcasted_iota(jnp.int32, (GN, HP), 0)
    sc = lax.broadcasted_iota(jnp.int32, (GN, HP), 1)
    smask = (sr // SSD_STATE) == (sc // (HP // SSD_GROUPS))
    GW = HP // SSD_GROUPS
    dr = lax.broadcasted_iota(jnp.int32, (GW, GW), 0)
    dc = lax.broadcasted_iota(jnp.int32, (GW, GW), 1)
    dmask = (dr // CHUNK) == (dc // SSD_HEAD_DIM)

    def chunk_body(c, carry):
        r0 = pl.multiple_of(c * CHUNK, CHUNK)
        rows = pl.ds(r0, CHUNK)
        xs = conv_ref[rows, 0:HP]
        Bc = conv_ref[rows, HP:HP + GN].astype(BF16)
        Cc = conv_ref[rows, HP + GN:HP + 2 * GN].astype(BF16)
        dtx = dtx_ref[rows, :]
        cs = csx_ref[rows, :]
        cs_row = r_ref[pl.ds(c, 1), :]
        cs_end = csx_ref[pl.ds(r0 + CHUNK - 1, 1), :]

        xdt = xs * dtx
        Lm = jnp.exp(jnp.where(tril, cs - cs_row, NEG_BIG))
        b_rep = jnp.where(bmask, jnp.concatenate([Bc] * SSD_HEADS, axis=0), jnp.zeros((), BF16))
        scores = _dot_nt(Cc, b_rep)
        G = (scores * Lm).astype(BF16)

        xdt_b = xdt.astype(BF16)
        y_parts = []
        for g in range(SSD_GROUPS):
            xg = xdt_b[:, g * GW:(g + 1) * GW]
            xbd = jnp.where(dmask, jnp.concatenate([xg] * (GW // CHUNK), axis=0),
                            jnp.zeros((), BF16))
            y_parts.append(_dot(G[:, g * GW:(g + 1) * GW], xbd))
        y = jnp.concatenate(y_parts, axis=1)

        st = state_ref[...]
        y = y + _dot(Cc, st.astype(BF16)) * jnp.exp(cs)
        xdec = (xdt * jnp.exp(cs_end - cs)).astype(BF16)
        upd = jnp.where(smask, _dot_tn(Bc, xdec), 0.0)
        state_ref[...] = st * jnp.exp(cs_end) + upd

        y = y + dx_ref[...] * xs
        y = y * _silu(z_ref[rows, :].astype(F32))
        o_ref[rows, :] = _rms(y, g_ref[...]).astype(o_ref.dtype)
        return carry

    lax.fori_loop(0, nc, chunk_body, 0)


def _ssd(xbc, z, dtx, r, conv_w, conv_b, alogx, dx, norm_g, B, S):
    nc = S // CHUNK
    b3 = lambda n: pl.BlockSpec((None, S, n), lambda b: (b, 0, 0))
    const = lambda shape: pl.BlockSpec(shape, lambda b: (0,) * len(shape))
    return pl.pallas_call(
        _ssd_kernel,
        grid=(B,),
        in_specs=[b3(SSD_CONV_CH), b3(SSD_WIDTH), b3(SSD_WIDTH),
                  pl.BlockSpec((None, nc, SSD_WIDTH), lambda b: (b, 0, 0)),
                  const((SSD_CONV, SSD_CONV_CH)), const((1, SSD_CONV_CH)),
                  const((1, SSD_WIDTH)), const((1, SSD_WIDTH)), const((1, SSD_WIDTH))],
        out_specs=b3(SSD_WIDTH),
        out_shape=jax.ShapeDtypeStruct((B, S, SSD_WIDTH), BF16),
        scratch_shapes=[pltpu.VMEM((S, SSD_CONV_CH), F32), pltpu.VMEM((S, SSD_WIDTH), F32),
                        pltpu.VMEM((SSD_GROUPS * SSD_STATE, SSD_WIDTH), F32)],
        compiler_params=_cparams("parallel"),
        name="ssd_mixer",
    )(xbc.reshape(B, S, SSD_CONV_CH), z.reshape(B, S, SSD_WIDTH), dtx.reshape(B, S, SSD_WIDTH),
      r, conv_w, conv_b, alogx, dx, norm_g)


def _memkv_kernel(mem_ref, g_ref, w_ref, kv_ref):
    kv_ref[...] = _dot(_rms(mem_ref[...], g_ref[...]).astype(BF16), w_ref[...]).astype(kv_ref.dtype)


def _memkv(mem, g_mem, w_xkv):
    B, M, D = mem.shape
    return pl.pallas_call(
        _memkv_kernel,
        grid=(B,),
        in_specs=[pl.BlockSpec((None, M, D), lambda b: (b, 0, 0)),
                  pl.BlockSpec((1, D), lambda b: (0, 0)),
                  pl.BlockSpec((D, 2 * X_WIDTH), lambda b: (0, 0))],
        out_specs=pl.BlockSpec((None, M, 2 * X_WIDTH), lambda b: (b, 0, 0)),
        out_shape=jax.ShapeDtypeStruct((B, M, 2 * X_WIDTH), BF16),
        compiler_params=_cparams("parallel"),
        name="mem_kv",
    )(mem, g_mem, w_xkv)


def _route(lg, le):
    lane = lax.broadcasted_iota(jnp.int32, lg.shape, 1)
    big = jnp.int32(LANES)
    lgm = jnp.where(lane < N_EGROUPS, lg, NEG_BIG)
    gmax = jnp.max(lgm, axis=-1, keepdims=True)
    g_p = 1.0 / jnp.sum(jnp.exp(lgm - gmax), axis=-1, keepdims=True)
    g_idx = jnp.min(jnp.where(lgm == gmax, lane, big), axis=-1, keepdims=True)
    in_grp = (lane // EXPERTS_PER_GROUP == g_idx) & (lane < N_EXPERTS)
    e1 = jnp.where(in_grp, le, NEG_BIG)
    v1 = jnp.max(e1, axis=-1, keepdims=True)
    i1 = jnp.min(jnp.where(e1 == v1, lane, big), axis=-1, keepdims=True)
    e2 = jnp.where(lane == i1, NEG_BIG, e1)
    v2 = jnp.max(e2, axis=-1, keepdims=True)
    i2 = jnp.min(jnp.where(e2 == v2, lane, big), axis=-1, keepdims=True)
    t = jnp.exp(v2 - v1)
    w1 = g_p / (1.0 + t)
    w2 = g_p * t / (1.0 + t)
    return lane, i1, i2, w1, w2


def _merge_kernel(x_ref, oa_ref, ob_ref, oc_ref, gate_ref, wb_ref, wmix_ref, gx_ref, wq_ref,
                  kv_ref, wo_ref, gm_ref, wr_ref, br_ref, x2_ref, hm_ref, comb_ref):
    D = x_ref.shape[1]
    merged = gate_ref[:, 0:D].astype(F32) * _dot(oa_ref[...], wb_ref[0])
    merged = merged + gate_ref[:, D:2 * D].astype(F32) * _dot(ob_ref[...], wb_ref[1])
    merged = merged + gate_ref[:, 2 * D:3 * D].astype(F32) * _dot(oc_ref[...], wb_ref[2])
    x1 = x_ref[...] + _dot(merged.astype(BF16), wmix_ref[...])

    hq = _rms(x1, gx_ref[...]).astype(BF16)
    q = (_dot(hq, wq_ref[...]) * (X_HEAD_DIM ** -0.5)).astype(BF16)
    k = kv_ref[:, 0:X_WIDTH]
    v = kv_ref[:, X_WIDTH:2 * X_WIDTH]
    qlane = lax.broadcasted_iota(jnp.int32, q.shape, 1) // X_HEAD_DIM
    vlane = lax.broadcasted_iota(jnp.int32, v.shape, 1) // X_HEAD_DIM
    o = jnp.zeros(q.shape, F32)
    for hd in range(X_HEADS):
        s = _dot_nt(jnp.where(qlane == hd, q, jnp.zeros_like(q)), k)
        p = jnp.exp(s - jnp.max(s, axis=-1, keepdims=True))
        p = p / jnp.sum(p, axis=-1, keepdims=True)
        o = o + _dot(p.astype(BF16), jnp.where(vlane == hd, v, jnp.zeros_like(v)))
    x2 = x1 + _dot(o.astype(BF16), wo_ref[...])
    x2_ref[...] = x2

    hm = _rms(x2, gm_ref[...]).astype(BF16)
    hm_ref[...] = hm
    logits = _dot(hm, wr_ref[...]) + br_ref[...]
    lane, i1, i2, w1, w2 = _route(logits[:, 0:LANES], logits[:, LANES:2 * LANES])
    comb_ref[...] = jnp.where(lane == i1, w1, 0.0) + jnp.where(lane == i2, w2, 0.0)


def _merge(x, oa, ob, oc, gates, wb, wmix, gx, wq, kv, wo, gm, wr, br, S):
    T, D = x.shape
    tm = TM_MERGE
    per_b = S // tm
    row = lambda n: pl.BlockSpec((tm, n), lambda i: (i, 0))
    const = lambda shape: pl.BlockSpec(shape, lambda i: (0,) * len(shape),
                                       pipeline_mode=pl.Buffered(1))
    return pl.pallas_call(
        _merge_kernel,
        grid=(T // tm,),
        in_specs=[row(D), row(512), row(512), row(512), row(3 * D),
                  const((N_BRANCH, 512, D)), const((D, D)), const((1, D)), const((D, X_WIDTH)),
                  pl.BlockSpec((None, N_MEM, 2 * X_WIDTH), lambda i: (i // per_b, 0, 0)),
                  const((X_WIDTH, D)), const((1, D)), const((D, 2 * LANES)), const((1, 2 * LANES))],
        out_specs=(row(D), row(D), row(LANES)),
        out_shape=(jax.ShapeDtypeStruct((T, D), F32), jax.ShapeDtypeStruct((T, D), BF16),
                   jax.ShapeDtypeStruct((T, LANES), F32)),
        compiler_params=_cparams("parallel"),
        name="merge_xattn_router",
    )(x, oa, ob, oc, gates, wb, wmix, gx, wq, kv, wo, gm, wr, br)


def _moe_kernel(x_ref, h_ref, comb_ref, wg_ref, wu_ref, wd_ref, gf_ref, o_ref, acc_ref, *, final):
    e = pl.program_id(1)

    @pl.when(e == 0)
    def _():
        acc_ref[...] = jnp.zeros(acc_ref.shape, F32)

    h = h_ref[...]
    a = _silu(_dot(h, wg_ref[...])) * _dot(h, wu_ref[...])
    lane = lax.broadcasted_iota(jnp.int32, comb_ref.shape, 1)
    w = jnp.sum(jnp.where(lane == e, comb_ref[...], 0.0), axis=-1, keepdims=True)
    acc_ref[...] += w * _dot(a.astype(BF16), wd_ref[...])

    @pl.when(e == N_EXPERTS - 1)
    def _():
        y = x_ref[...] + acc_ref[...]
        o_ref[...] = _rms(y, gf_ref[...]) if final else y


def _moe(x2, hm, comb, wg, wu, wd, g_final, final):
    T, D = x2.shape
    tm = TM_MOE
    row = lambda n: pl.BlockSpec((tm, n), lambda i, e: (i, 0))
    return pl.pallas_call(
        functools.partial(_moe_kernel, final=final),
        grid=(T // tm, N_EXPERTS),
        in_specs=[row(D), row(D), row(LANES),
                  pl.BlockSpec((None, D, EXPERT_FF), lambda i, e: (e, 0, 0)),
                  pl.BlockSpec((None, D, EXPERT_FF), lambda i, e: (e, 0, 0)),
                  pl.BlockSpec((None, EXPERT_FF, D), lambda i, e: (e, 0, 0)),
                  pl.BlockSpec((1, D), lambda i, e: (0, 0))],
        out_specs=row(D),
        out_shape=jax.ShapeDtypeStruct((T, D), F32),
        scratch_shapes=[pltpu.VMEM((tm, D), F32)],
        compiler_params=_cparams("parallel", "arbitrary"),
        name="moe_experts",
    )(x2, hm, comb, wg, wu, wd, g_final)


def _prep_inproj_weight(w_in_l):
    col_dt = 2560 + SSD_CONV_CH
    w_dt = w_in_l[:, col_dt:col_dt + SSD_HEADS]
    return jnp.concatenate(
        [w_in_l[:, :col_dt], w_in_l[:, col_dt + SSD_HEADS:],
         jnp.repeat(w_dt, SSD_HEAD_DIM, axis=1),
         jnp.pad(w_dt, ((0, 0), (0, LANES - SSD_HEADS)))], axis=1).astype(BF16)


def _pad_lanes(v, n=LANES):
    return jnp.pad(v, (0, n - v.shape[0]))[None, :]


def kernel(x, mem, w_in, b_gate, lam_q1, lam_k1, lam_q2, lam_k2, attn_subln_g, rel_bias_table,
           w_pool_group, pool_scale, ssd_conv_w, ssd_conv_b, ssd_dt_bias, ssd_A_log, ssd_D,
           ssd_norm_g, w_branch, w_mix_out, g_mix, g_xattn, g_mem, w_xq, w_xkv, w_xo, g_moe,
           w_router_group, b_router_group, w_router_expert, b_router_expert,
           w_exp_gate, w_exp_up, w_exp_down, g_final):
    B, S, D = x.shape
    T = B * S
    depth = w_in.shape[0]
    nc = S // CHUNK
    lam_inits = tuple(0.8 - 0.6 * math.exp(-0.3 * l) for l in range(depth))
    bias, lam = _setup(rel_bias_table, lam_q1, lam_k1, lam_q2, lam_k2, lam_inits)

    xf = x.reshape(T, D)
    for l in range(depth):
        w = _prep_inproj_weight(w_in[l])
        qkv, pool_u, z, xbc, gates, dtx, cs8 = _inproj(
            xf, g_mix[l][None, :], w, b_gate[l][None, :],
            jnp.repeat(ssd_dt_bias[l], SSD_HEAD_DIM)[None, :], _pad_lanes(ssd_dt_bias[l]),
            _pad_lanes(ssd_A_log[l]))
        o_a = _attention(qkv, bias, lam[l][None, :], attn_subln_g[l][None, :],
                         1.0 - lam_inits[l], B, S)
        o_b = _pool(pool_u, w_pool_group[l].astype(BF16), pool_scale[l][None, :], B, S)
        r = cs8[:, :SSD_HEADS].reshape(B, nc, CHUNK, SSD_HEADS).transpose(0, 1, 3, 2)
        r = r.reshape(B, nc, SSD_WIDTH)
        o_c = _ssd(xbc, z, dtx, r, ssd_conv_w[l], ssd_conv_b[l][None, :],
                   jnp.repeat(ssd_A_log[l], SSD_HEAD_DIM)[None, :],
                   jnp.repeat(ssd_D[l], SSD_HEAD_DIM)[None, :], ssd_norm_g[l][None, :], B, S)
        kv = _memkv(mem, g_mem[l][None, :], w_xkv[l].astype(BF16))
        w_r = jnp.concatenate(
            [jnp.pad(w_router_group[l], ((0, 0), (0, LANES - N_EGROUPS))),
             jnp.pad(w_router_expert[l], ((0, 0), (0, LANES - N_EXPERTS)))], axis=1).astype(BF16)
        b_r = jnp.concatenate([_pad_lanes(b_router_group[l]), _pad_lanes(b_router_expert[l])],
                              axis=1)
        x2, hm, comb = _merge(
            xf, o_a.reshape(T, A_WIDTH), o_b.reshape(T, POOL_WIDTH), o_c.reshape(T, SSD_WIDTH),
            gates, w_branch[l].astype(BF16), w_mix_out[l].astype(BF16), g_xattn[l][None, :],
            w_xq[l].astype(BF16), kv, w_xo[l].astype(BF16), g_moe[l][None, :], w_r, b_r, S)
        xf = _moe(x2, hm, comb, w_exp_gate[l].astype(BF16), w_exp_up[l].astype(BF16),
                  w_exp_down[l].astype(BF16), g_final[None, :], l == depth - 1)
    return xf.reshape(B, S, D)
```

```python
import functools
import math

import jax
import jax.numpy as jnp
from jax import lax
from jax.experimental import pallas as pl
from jax.experimental.pallas import tpu as pltpu

F32 = jnp.float32
BF16 = jnp.bfloat16

EPS = 1e-6
CHUNK = 64
N_MEM = 256

A_HEADS = 4
A_QK_DIM = 64
A_V_DIM = 128
A_WIDTH = 512
REL_BUCKETS = 32
REL_MAX_DIST = 128

POOL_WINDOWS = (2, 4, 8, 16)
POOL_GROUP_DIM = 128
POOL_WIDTH = 512

SSD_HEADS = 8
SSD_HEAD_DIM = 64
SSD_WIDTH = 512
SSD_GROUPS = 2
SSD_STATE = 64
SSD_CONV = 4
SSD_CONV_CH = 768

N_BRANCH = 3
X_HEADS = 4
X_HEAD_DIM = 64
X_WIDTH = 256

N_EGROUPS = 4
EXPERTS_PER_GROUP = 4
N_EXPERTS = 16
EXPERT_FF = 512

LANES = 128
NEG_BIG = -1e30

C_Q, C_K, C_V, C_POOL, C_Z, C_XBC, C_GATE = 0, 512, 1024, 1536, 2048, 2560, 3328
C_DTX = 6400
C_DT8 = 6912
C_END = 7040

TM_PROJ = 512
TM_MERGE = 512
TM_MOE = 512
TA = 256
VMEM_LIMIT = 56 * 1024 * 1024


def _cparams(*sem):
    return pltpu.CompilerParams(dimension_semantics=sem, vmem_limit_bytes=VMEM_LIMIT)


def _rms(x, g):
    return x * lax.rsqrt(jnp.mean(x * x, axis=-1, keepdims=True) + EPS) * g


def _dot(a, b):
    return jnp.dot(a, b, preferred_element_type=F32)


def _dot_nt(a, b):
    return lax.dot_general(a, b, (((1,), (1,)), ((), ())), preferred_element_type=F32)


def _dot_tn(a, b):
    return lax.dot_general(a, b, (((0,), (0,)), ((), ())), preferred_element_type=F32)


def _sigmoid(x):
    return 1.0 / (1.0 + jnp.exp(-x))


def _silu(x):
    return x * _sigmoid(x)


def _softplus(x):
    return jnp.maximum(x, 0.0) + jnp.log(1.0 + jnp.exp(-jnp.abs(x)))


def _shift_rows(v, s, row):
    return jnp.where(row >= s, pltpu.roll(v, s, axis=0), 0.0)


def _chunk_cumsum(v, row_in_chunk):
    s = 1
    while s < CHUNK:
        v = v + jnp.where(row_in_chunk >= s, pltpu.roll(v, s, axis=0), 0.0)
        s *= 2
    return v


def _setup_kernel(table_ref, bucket_ref, vis_ref, lq1_ref, lk1_ref, lq2_ref, lk2_ref,
                  bias_ref, lam_ref, *, lam_inits):
    for h in range(A_HEADS):
        tiles = []
        for d in range(3):
            bkt = bucket_ref[d]
            acc = jnp.zeros(bkt.shape, F32)
            for b in range(REL_BUCKETS):
                acc = acc + jnp.where(bkt == b, table_ref[b, h], 0.0)
            tiles.append(acc)
        bias_ref[h, 0] = jnp.where(vis_ref[...] > 0, tiles[0] - tiles[2], NEG_BIG)
        bias_ref[h, 1] = tiles[1] - tiles[2]
    s1 = jnp.sum(lq1_ref[...] * lk1_ref[...], axis=-1, keepdims=True)
    s2 = jnp.sum(lq2_ref[...] * lk2_ref[...], axis=-1, keepdims=True)
    row = lax.broadcasted_iota(jnp.int32, s1.shape, 0)
    init = jnp.zeros(s1.shape, F32)
    for l, v in enumerate(lam_inits):
        init = jnp.where(row == l, v, init)
    lam_ref[...] = jnp.broadcast_to(jnp.exp(s1) - jnp.exp(s2) + init, lam_ref.shape)


def _rel_bucket_index(rel):
    nb = REL_BUCKETS // 2
    max_exact = nb // 2
    side = jnp.where(rel > 0, nb, 0)
    n = jnp.abs(rel)
    n_f = jnp.maximum(n, 1).astype(F32)
    large = max_exact + (jnp.log(n_f / max_exact) / math.log(REL_MAX_DIST / max_exact)
                         * (nb - max_exact)).astype(jnp.int32)
    large = jnp.minimum(large, nb - 1)
    return side + jnp.where(n < max_exact, n, large)


def _setup(rel_bias_table, lam_q1, lam_k1, lam_q2, lam_k2, lam_inits):
    depth = lam_q1.shape[0]
    r = jnp.arange(TA)
    rel0 = r[None, :] - r[:, None]
    bucket = jnp.stack([_rel_bucket_index(rel0 - d * TA) for d in range(3)]).astype(jnp.int32)
    vis = ((r[None, :] // CHUNK) <= (r[:, None] // CHUNK)).astype(jnp.int32)
    vm = pl.BlockSpec(memory_space=pltpu.VMEM)
    return pl.pallas_call(
        functools.partial(_setup_kernel, lam_inits=lam_inits),
        out_shape=(jax.ShapeDtypeStruct((A_HEADS, 2, TA, TA), F32),
                   jax.ShapeDtypeStruct((depth, LANES), F32)),
        in_specs=[pl.BlockSpec(memory_space=pltpu.SMEM), vm, vm, vm, vm, vm, vm],
        out_specs=(vm, vm),
        name="setup_bias_lam",
    )(rel_bias_table, bucket, vis, lam_q1, lam_k1, lam_q2, lam_k2)


def _inproj_kernel(x_ref, g_ref, w_ref, bg_ref, dtbx_ref, dtb8_ref, alog8_ref,
                   qkv_ref, pool_ref, z_ref, xbc_ref, gate_ref, dtx_ref, cs8_ref):
    h = _rms(x_ref[...], g_ref[...]).astype(BF16)

    def mm(lo, hi):
        return _dot(h, w_ref[:, lo:hi])

    qkv_ref[:, 0:512] = (mm(C_Q, C_K) * (A_QK_DIM ** -0.5)).astype(BF16)
    qkv_ref[:, 512:1024] = mm(C_K, C_V).astype(BF16)
    qkv_ref[:, 1024:1536] = mm(C_V, C_POOL).astype(BF16)
    pool_ref[...] = mm(C_POOL, C_Z).astype(BF16)
    z_ref[...] = mm(C_Z, C_XBC).astype(BF16)
    xbc_ref[:, 0:512] = mm(C_XBC, C_XBC + 512).astype(BF16)
    xbc_ref[:, 512:768] = mm(C_XBC + 512, C_GATE).astype(BF16)
    for j in range(6):
        lo = C_GATE + 512 * j
        gate_ref[:, 512 * j:512 * (j + 1)] = _sigmoid(
            mm(lo, lo + 512) + bg_ref[:, 512 * j:512 * (j + 1)]).astype(BF16)
    dtx_ref[...] = _softplus(mm(C_DTX, C_DT8) + dtbx_ref[...])
    dt8 = _softplus(mm(C_DT8, C_END) + dtb8_ref[...])
    row = lax.broadcasted_iota(jnp.int32, dt8.shape, 0) % CHUNK
    cs8_ref[...] = _chunk_cumsum(dt8 * (-jnp.exp(alog8_ref[...])), row)


def _inproj(x, g, w, bg, dtbx, dtb8, alog8):
    T, D = x.shape
    tm = TM_PROJ
    const = lambda shape: pl.BlockSpec(shape, lambda i: (0, 0), pipeline_mode=pl.Buffered(1))
    row = lambda n: pl.BlockSpec((tm, n), lambda i: (i, 0))
    return pl.pallas_call(
        _inproj_kernel,
        grid=(T // tm,),
        in_specs=[row(D), const((1, D)), const((D, C_END)), const((1, 3072)),
                  const((1, 512)), const((1, LANES)), const((1, LANES))],
        out_specs=(row(1536), row(512), row(512), row(768), row(3072), row(512), row(LANES)),
        out_shape=(jax.ShapeDtypeStruct((T, 1536), BF16), jax.ShapeDtypeStruct((T, 512), BF16),
                   jax.ShapeDtypeStruct((T, 512), BF16), jax.ShapeDtypeStruct((T, 768), BF16),
                   jax.ShapeDtypeStruct((T, 3072), BF16), jax.ShapeDtypeStruct((T, 512), F32),
                   jax.ShapeDtypeStruct((T, LANES), F32)),
        compiler_params=_cparams("parallel"),
        name="inproj",
    )(x, g, w, bg, dtbx, dtb8, alog8)


def _attn_kernel(q_ref, k_ref, v_ref, bias_ref, lam_ref, g_ref, o_ref, *, out_scale):
    S = q_ref.shape[0]
    lam = lam_ref[...]
    g = g_ref[...]
    for i in range(S // TA):
        q = q_ref[i * TA:(i + 1) * TA, :]
        lane = lax.broadcasted_iota(jnp.int32, q.shape, 1)
        qs = (jnp.where(lane < A_QK_DIM, q, jnp.zeros_like(q)),
              jnp.where(lane >= A_QK_DIM, q, jnp.zeros_like(q)))
        pieces = []
        if i >= 2:
            pieces.append((0, (i - 1) * TA, None))
        if i >= 1:
            pieces.append(((i - 1) * TA, i * TA, 1))
        pieces.append((i * TA, (i + 1) * TA, 0))
        outs = []
        for qm in qs:
            ss = []
            for lo, hi, bi in pieces:
                s = _dot_nt(qm, k_ref[lo:hi, :])
                if bi is not None:
                    s = s + bias_ref[bi]
                ss.append(s)
            m = jnp.max(ss[0], axis=-1, keepdims=True)
            for s in ss[1:]:
                m = jnp.maximum(m, jnp.max(s, axis=-1, keepdims=True))
            l = jnp.zeros((TA, 1), F32)
            acc = jnp.zeros((TA, A_V_DIM), F32)
            for s, (lo, hi, _) in zip(ss, pieces):
                p = jnp.exp(s - m)
                l = l + jnp.sum(p, axis=-1, keepdims=True)
                acc = acc + _dot(p.astype(BF16), v_ref[lo:hi, :])
            outs.append(acc / l)
        o = outs[0] - lam * outs[1]
        o_ref[i * TA:(i + 1) * TA, :] = (_rms(o, g) * out_scale).astype(o_ref.dtype)


def _attention(qkv, bias, lam_row, subln_g, out_scale, B, S):
    qkv3 = qkv.reshape(B, S, 3 * A_WIDTH)
    seq = lambda off: pl.BlockSpec((None, S, LANES), lambda b, h: (b, 0, off + h))
    return pl.pallas_call(
        functools.partial(_attn_kernel, out_scale=out_scale),
        grid=(B, A_HEADS),
        in_specs=[seq(0), seq(A_HEADS), seq(2 * A_HEADS),
                  pl.BlockSpec((None, 2, TA, TA), lambda b, h: (h, 0, 0, 0)),
                  pl.BlockSpec((1, LANES), lambda b, h: (0, 0)),
                  pl.BlockSpec((1, LANES), lambda b, h: (0, 0))],
        out_specs=seq(0),
        out_shape=jax.ShapeDtypeStruct((B, S, A_WIDTH), BF16),
        compiler_params=_cparams("parallel", "parallel"),
        name="diff_attention",
    )(qkv3, qkv3, qkv3, bias, lam_row, subln_g)


def _pool_kernel(u_ref, w_ref, scale_ref, o_ref):
    S = u_ref.shape[0]
    row = lax.broadcasted_iota(jnp.int32, (S, POOL_GROUP_DIM), 0)
    t = (row + 1).astype(F32)
    for gi, win in enumerate(POOL_WINDOWS):
        sl = slice(gi * POOL_GROUP_DIM, (gi + 1) * POOL_GROUP_DIM)
        u = u_ref[:, sl].astype(F32)
        acc = u
        s = 1
        while s < win:
            acc = acc + _shift_rows(acc, s, row)
            s *= 2
        d = acc / jnp.minimum(t, float(win)) - u
        y = _dot(d.astype(BF16), w_ref[gi])
        o_ref[:, sl] = (y * scale_ref[:, sl]).astype(o_ref.dtype)


def _pool(u, w_group, scale, B, S):
    u3 = u.reshape(B, S, POOL_WIDTH)
    return pl.pallas_call(
        _pool_kernel,
        grid=(B,),
        in_specs=[pl.BlockSpec((None, S, POOL_WIDTH), lambda b: (b, 0, 0)),
                  pl.BlockSpec((len(POOL_WINDOWS), POOL_GROUP_DIM, POOL_GROUP_DIM),
                               lambda b: (0, 0, 0)),
                  pl.BlockSpec((1, POOL_WIDTH), lambda b: (0, 0))],
        out_specs=pl.BlockSpec((None, S, POOL_WIDTH), lambda b: (b, 0, 0)),
        out_shape=jax.ShapeDtypeStruct((B, S, POOL_WIDTH), BF16),
        compiler_params=_cparams("parallel"),
        name="pool_mixer",
    )(u3, w_group, scale)


def _ssd_kernel(xbc_ref, z_ref, dtx_ref, r_ref, convw_ref, convb_ref, alogx_ref, dx_ref, g_ref,
                o_ref, conv_ref, csx_ref, state_ref):
    S = xbc_ref.shape[0]
    nc = S // CHUNK
    GN = SSD_GROUPS * SSD_STATE
    HP = SSD_WIDTH

    xin = xbc_ref[...].astype(F32)
    row = lax.broadcasted_iota(jnp.int32, xin.shape, 0)
    acc = xin * convw_ref[SSD_CONV - 1:SSD_CONV, :] + convb_ref[...]
    for s in range(1, SSD_CONV):
        acc = acc + _shift_rows(xin, s, row) * convw_ref[SSD_CONV - 1 - s:SSD_CONV - s, :]
    conv_ref[...] = _silu(acc)

    row_c = lax.broadcasted_iota(jnp.int32, (S, HP), 0) % CHUNK
    csx_ref[...] = _chunk_cumsum(dtx_ref[...] * (-jnp.exp(alogx_ref[...])), row_c)

    state_ref[...] = jnp.zeros(state_ref.shape, F32)

    li = lax.broadcasted_iota(jnp.int32, (CHUNK, HP), 0)
    ci = lax.broadcasted_iota(jnp.int32, (CHUNK, HP), 1)
    tril = li >= (ci % CHUNK)
    br = lax.broadcasted_iota(jnp.int32, (HP, GN), 0)
    bc = lax.broadcasted_iota(jnp.int32, (HP, GN), 1)
    bmask = (br // (HP // SSD_GROUPS)) == (bc // SSD_STATE)
    sr = lax.broadcasted_iota(jnp.int32, (GN, HP), 0)
    sc = lax.broadcasted_iota(jnp.int32, (GN, HP), 1)
    smask = (sr // SSD_STATE) == (sc // (HP // SSD_GROUPS))
    GW = HP // SSD_GROUPS
    dr = lax.broadcasted_iota(jnp.int32, (GW, GW), 0)
    dc = lax.broadcasted_iota(jnp.int32, (GW, GW), 1)
    dmask = (dr // CHUNK) == (dc // SSD_HEAD_DIM)

    def chunk_body(c, carry):
        r0 = pl.multiple_of(c * CHUNK, CHUNK)
        rows = pl.ds(r0, CHUNK)
        xs = conv_ref[rows, 0:HP]
        Bc = conv_ref[rows, HP:HP + GN].astype(BF16)
        Cc = conv_ref[rows, HP + GN:HP + 2 * GN].astype(BF16)
        dtx = dtx_ref[rows, :]
        cs = csx_ref[rows, :]
        cs_row = r_ref[pl.ds(c, 1), :]
        cs_end = csx_ref[pl.ds(r0 + CHUNK - 1, 1), :]

        xdt = xs * dtx
        Lm = jnp.exp(jnp.where(tril, cs - cs_row, NEG_BIG))
        b_rep = jnp.where(bmask, jnp.concatenate([Bc] * SSD_HEADS, axis=0), jnp.zeros((), BF16))
        scores = _dot_nt(Cc, b_rep)
        G = (scores * Lm).astype(BF16)

        xdt_b = xdt.astype(BF16)
        y_parts = []
        for g in range(SSD_GROUPS):
            xg = xdt_b[:, g * GW:(g + 1) * GW]
            xbd = jnp.where(dmask, jnp.concatenate([xg] * (GW // CHUNK), axis=0),
                            jnp.zeros((), BF16))
            y_parts.append(_dot(G[:, g * GW:(g + 1) * GW], xbd))
        y = jnp.concatenate(y_parts, axis=1)

        st = state_ref[...]
        y = y + _dot(Cc, st.astype(BF16)) * jnp.exp(cs)
        xdec = (xdt * jnp.exp(cs_end - cs)).astype(BF16)
        upd = jnp.where(smask, _dot_tn(Bc, xdec), 0.0)
        state_ref[...] = st * jnp.exp(cs_end) + upd

        y = y + dx_ref[...] * xs
        y = y * _silu(z_ref[rows, :].astype(F32))
        o_ref[rows, :] = _rms(y, g_ref[...]).astype(o_ref.dtype)
        return carry

    lax.fori_loop(0, nc, chunk_body, 0)


def _ssd(xbc, z, dtx, r, conv_w, conv_b, alogx, dx, norm_g, B, S):
    nc = S // CHUNK
    b3 = lambda n: pl.BlockSpec((None, S, n), lambda b: (b, 0, 0))
    const = lambda shape: pl.BlockSpec(shape, lambda b: (0,) * len(shape))
    return pl.pallas_call(
        _ssd_kernel,
        grid=(B,),
        in_specs=[b3(SSD_CONV_CH), b3(SSD_WIDTH), b3(SSD_WIDTH),
                  pl.BlockSpec((None, nc, SSD_WIDTH), lambda b: (b, 0, 0)),
                  const((SSD_CONV, SSD_CONV_CH)), const((1, SSD_CONV_CH)),
                  const((1, SSD_WIDTH)), const((1, SSD_WIDTH)), const((1, SSD_WIDTH))],
        out_specs=b3(SSD_WIDTH),
        out_shape=jax.ShapeDtypeStruct((B, S, SSD_WIDTH), BF16),
        scratch_shapes=[pltpu.VMEM((S, SSD_CONV_CH), F32), pltpu.VMEM((S, SSD_WIDTH), F32),
                        pltpu.VMEM((SSD_GROUPS * SSD_STATE, SSD_WIDTH), F32)],
        compiler_params=_cparams("parallel"),
        name="ssd_mixer",
    )(xbc.reshape(B, S, SSD_CONV_CH), z.reshape(B, S, SSD_WIDTH), dtx.reshape(B, S, SSD_WIDTH),
      r, conv_w, conv_b, alogx, dx, norm_g)


def _memkv_kernel(mem_ref, g_ref, w_ref, kv_ref):
    kv_ref[...] = _dot(_rms(mem_ref[...], g_ref[...]).astype(BF16), w_ref[...]).astype(kv_ref.dtype)


def _memkv(mem, g_mem, w_xkv):
    B, M, D = mem.shape
    return pl.pallas_call(
        _memkv_kernel,
        grid=(B,),
        in_specs=[pl.BlockSpec((None, M, D), lambda b: (b, 0, 0)),
                  pl.BlockSpec((1, D), lambda b: (0, 0)),
                  pl.BlockSpec((D, 2 * X_WIDTH), lambda b: (0, 0))],
        out_specs=pl.BlockSpec((None, M, 2 * X_WIDTH), lambda b: (b, 0, 0)),
        out_shape=jax.ShapeDtypeStruct((B, M, 2 * X_WIDTH), BF16),
        compiler_params=_cparams("parallel"),
        name="mem_kv",
    )(mem, g_mem, w_xkv)


def _route(lg, le):
    lane = lax.broadcasted_iota(jnp.int32, lg.shape, 1)
    big = jnp.int32(LANES)
    lgm = jnp.where(lane < N_EGROUPS, lg, NEG_BIG)
    gmax = jnp.max(lgm, axis=-1, keepdims=True)
    g_p = 1.0 / jnp.sum(jnp.exp(lgm - gmax), axis=-1, keepdims=True)
    g_idx = jnp.min(jnp.where(lgm == gmax, lane, big), axis=-1, keepdims=True)
    in_grp = (lane // EXPERTS_PER_GROUP == g_idx) & (lane < N_EXPERTS)
    e1 = jnp.where(in_grp, le, NEG_BIG)
    v1 = jnp.max(e1, axis=-1, keepdims=True)
    i1 = jnp.min(jnp.where(e1 == v1, lane, big), axis=-1, keepdims=True)
    e2 = jnp.where(lane == i1, NEG_BIG, e1)
    v2 = jnp.max(e2, axis=-1, keepdims=True)
    i2 = jnp.min(jnp.where(e2 == v2, lane, big), axis=-1, keepdims=True)
    t = jnp.exp(v2 - v1)
    w1 = g_p / (1.0 + t)
    w2 = g_p * t / (1.0 + t)
    return lane, i1, i2, w1, w2


def _merge_kernel(x_ref, oa_ref, ob_ref, oc_ref, gate_ref, wb_ref, wmix_ref, gx_ref, wq_ref,
                  kv_ref, wo_ref, gm_ref, wr_ref, br_ref, x2_ref, hm_ref, comb_ref):
    D = x_ref.shape[1]
    merged = gate_ref[:, 0:D].astype(F32) * _dot(oa_ref[...], wb_ref[0])
    merged = merged + gate_ref[:, D:2 * D].astype(F32) * _dot(ob_ref[...], wb_ref[1])
    merged = merged + gate_ref[:, 2 * D:3 * D].astype(F32) * _dot(oc_ref[...], wb_ref[2])
    x1 = x_ref[...] + _dot(merged.astype(BF16), wmix_ref[...])

    hq = _rms(x1, gx_ref[...]).astype(BF16)
    q = (_dot(hq, wq_ref[...]) * (X_HEAD_DIM ** -0.5)).astype(BF16)
    k = kv_ref[:, 0:X_WIDTH]
    v = kv_ref[:, X_WIDTH:2 * X_WIDTH]
    qlane = lax.broadcasted_iota(jnp.int32, q.shape, 1) // X_HEAD_DIM
    vlane = lax.broadcasted_iota(jnp.int32, v.shape, 1) // X_HEAD_DIM
    o = jnp.zeros(q.shape, F32)
    for hd in range(X_HEADS):
        s = _dot_nt(jnp.where(qlane == hd, q, jnp.zeros_like(q)), k)
        p = jnp.exp(s - jnp.max(s, axis=-1, keepdims=True))
        p = p / jnp.sum(p, axis=-1, keepdims=True)
        o = o + _dot(p.astype(BF16), jnp.where(vlane == hd, v, jnp.zeros_like(v)))
    x2 = x1 + _dot(o.astype(BF16), wo_ref[...])
    x2_ref[...] = x2

    hm = _rms(x2, gm_ref[...]).astype(BF16)
    hm_ref[...] = hm
    logits = _dot(hm, wr_ref[...]) + br_ref[...]
    lane, i1, i2, w1, w2 = _route(logits[:, 0:LANES], logits[:, LANES:2 * LANES])
    comb_ref[...] = jnp.where(lane == i1, w1, 0.0) + jnp.where(lane == i2, w2, 0.0)


def _merge(x, oa, ob, oc, gates, wb, wmix, gx, wq, kv, wo, gm, wr, br, S):
    T, D = x.shape
    tm = TM_MERGE
    per_b = S // tm
    row = lambda n: pl.BlockSpec((tm, n), lambda i: (i, 0))
    const = lambda shape: pl.BlockSpec(shape, lambda i: (0,) * len(shape),
                                       pipeline_mode=pl.Buffered(1))
    return pl.pallas_call(
        _merge_kernel,
        grid=(T // tm,),
        in_specs=[row(D), row(512), row(512), row(512), row(3 * D),
                  const((N_BRANCH, 512, D)), const((D, D)), const((1, D)), const((D, X_WIDTH)),
                  pl.BlockSpec((None, N_MEM, 2 * X_WIDTH), lambda i: (i // per_b, 0, 0)),
                  const((X_WIDTH, D)), const((1, D)), const((D, 2 * LANES)), const((1, 2 * LANES))],
        out_specs=(row(D), row(D), row(LANES)),
        out_shape=(jax.ShapeDtypeStruct((T, D), F32), jax.ShapeDtypeStruct((T, D), BF16),
                   jax.ShapeDtypeStruct((T, LANES), F32)),
        compiler_params=_cparams("parallel"),
        name="merge_xattn_router",
    )(x, oa, ob, oc, gates, wb, wmix, gx, wq, kv, wo, gm, wr, br)


def _moe_kernel(x_ref, h_ref, comb_ref, wg_ref, wu_ref, wd_ref, gf_ref, o_ref, acc_ref, *, final):
    e = pl.program_id(1)

    @pl.when(e == 0)
    def _():
        acc_ref[...] = jnp.zeros(acc_ref.shape, F32)

    h = h_ref[...]
    a = _silu(_dot(h, wg_ref[...])) * _dot(h, wu_ref[...])
    lane = lax.broadcasted_iota(jnp.int32, comb_ref.shape, 1)
    w = jnp.sum(jnp.where(lane == e, comb_ref[...], 0.0), axis=-1, keepdims=True)
    acc_ref[...] += w * _dot(a.astype(BF16), wd_ref[...])

    @pl.when(e == N_EXPERTS - 1)
    def _():
        y = x_ref[...] + acc_ref[...]
        o_ref[...] = _rms(y, gf_ref[...]) if final else y


def _moe(x2, hm, comb, wg, wu, wd, g_final, final):
    T, D = x2.shape
    tm = TM_MOE
    row = lambda n: pl.BlockSpec((tm, n), lambda i, e: (i, 0))
    return pl.pallas_call(
        functools.partial(_moe_kernel, final=final),
        grid=(T // tm, N_EXPERTS),
        in_specs=[row(D), row(D), row(LANES),
                  pl.BlockSpec((None, D, EXPERT_FF), lambda i, e: (e, 0, 0)),
                  pl.BlockSpec((None, D, EXPERT_FF), lambda i, e: (e, 0, 0)),
                  pl.BlockSpec((None, EXPERT_FF, D), lambda i, e: (e, 0, 0)),
                  pl.BlockSpec((1, D), lambda i, e: (0, 0))],
        out_specs=row(D),
        out_shape=jax.ShapeDtypeStruct((T, D), F32),
        scratch_shapes=[pltpu.VMEM((tm, D), F32)],
        compiler_params=_cparams("parallel", "arbitrary"),
        name="moe_experts",
    )(x2, hm, comb, wg, wu, wd, g_final)


def _prep_inproj_weight(w_in_l):
    col_dt = 2560 + SSD_CONV_CH
    w_dt = w_in_l[:, col_dt:col_dt + SSD_HEADS]
    return jnp.concatenate(
        [w_in_l[:, :col_dt], w_in_l[:, col_dt + SSD_HEADS:],
         jnp.repeat(w_dt, SSD_HEAD_DIM, axis=1),
         jnp.pad(w_dt, ((0, 0), (0, LANES - SSD_HEADS)))], axis=1).astype(BF16)


def _pad_lanes(v, n=LANES):
    return jnp.pad(v, (0, n - v.shape[0]))[None, :]


def kernel(x, mem, w_in, b_gate, lam_q1, lam_k1, lam_q2, lam_k2, attn_subln_g, rel_bias_table,
           w_pool_group, pool_scale, ssd_conv_w, ssd_conv_b, ssd_dt_bias, ssd_A_log, ssd_D,
           ssd_norm_g, w_branch, w_mix_out, g_mix, g_xattn, g_mem, w_xq, w_xkv, w_xo, g_moe,
           w_router_group, b_router_group, w_router_expert, b_router_expert,
           w_exp_gate, w_exp_up, w_exp_down, g_final):
    B, S, D = x.shape
    T = B * S
    depth = w_in.shape[0]
    nc = S // CHUNK
    lam_inits = tuple(0.8 - 0.6 * math.exp(-0.3 * l) for l in range(depth))
    bias, lam = _setup(rel_bias_table, lam_q1, lam_k1, lam_q2, lam_k2, lam_inits)

    xf = x.reshape(T, D)
    for l in range(depth):
        w = _prep_inproj_weight(w_in[l])
        qkv, pool_u, z, xbc, gates, dtx, cs8 = _inproj(
            xf, g_mix[l][None, :], w, b_gate[l][None, :],
            jnp.repeat(ssd_dt_bias[l], SSD_HEAD_DIM)[None, :], _pad_lanes(ssd_dt_bias[l]),
            _pad_lanes(ssd_A_log[l]))
        o_a = _attention(qkv, bias, lam[l][None, :], attn_subln_g[l][None, :],
                         1.0 - lam_inits[l], B, S)
        o_b = _pool(pool_u, w_pool_group[l].astype(BF16), pool_scale[l][None, :], B, S)
        r = cs8[:, :SSD_HEADS].reshape(B, nc, CHUNK, SSD_HEADS).transpose(0, 1, 3, 2)
        r = r.reshape(B, nc, SSD_WIDTH)
        o_c = _ssd(xbc, z, dtx, r, ssd_conv_w[l], ssd_conv_b[l][None, :],
                   jnp.repeat(ssd_A_log[l], SSD_HEAD_DIM)[None, :],
                   jnp.repeat(ssd_D[l], SSD_HEAD_DIM)[None, :], ssd_norm_g[l][None, :], B, S)
        kv = _memkv(mem, g_mem[l][None, :], w_xkv[l].astype(BF16))
        w_r = jnp.concatenate(
            [jnp.pad(w_router_group[l], ((0, 0), (0, LANES - N_EGROUPS))),
             jnp.pad(w_router_expert[l], ((0, 0), (0, LANES - N_EXPERTS)))], axis=1).astype(BF16)
        b_r = jnp.concatenate([_pad_lanes(b_router_group[l]), _pad_lanes(b_router_expert[l])],
                              axis=1)
        x2, hm, comb = _merge(
            xf, o_a.reshape(T, A_WIDTH), o_b.reshape(T, POOL_WIDTH), o_c.reshape(T, SSD_WIDTH),
            gates, w_branch[l].astype(BF16), w_mix_out[l].astype(BF16), g_xattn[l][None, :],
            w_xq[l].astype(BF16), kv, w_xo[l].astype(BF16), g_moe[l][None, :], w_r, b_r, S)
        xf = _moe(x2, hm, comb, w_exp_gate[l].astype(BF16), w_exp_up[l].astype(BF16),
                  w_exp_down[l].astype(BF16), g_final[None, :], l == depth - 1)
    return xf.reshape(B, S, D)
```

```python
import functools
import math

import jax
import jax.numpy as jnp
from jax import lax
from jax.experimental import pallas as pl
from jax.experimental.pallas import tpu as pltpu

F32 = jnp.float32
BF16 = jnp.bfloat16

EPS = 1e-6
CHUNK = 64
N_MEM = 256

A_HEADS = 4
A_QK_DIM = 64
A_V_DIM = 128
A_WIDTH = 512
REL_BUCKETS = 32
REL_MAX_DIST = 128

POOL_WINDOWS = (2, 4, 8, 16)
POOL_GROUP_DIM = 128
POOL_WIDTH = 512

SSD_HEADS = 8
SSD_HEAD_DIM = 64
SSD_WIDTH = 512
SSD_GROUPS = 2
SSD_STATE = 64
SSD_CONV = 4
SSD_CONV_CH = 768

N_BRANCH = 3
X_HEADS = 4
X_HEAD_DIM = 64
X_WIDTH = 256

N_EGROUPS = 4
EXPERTS_PER_GROUP = 4
N_EXPERTS = 16
EXPERT_FF = 512

LANES = 128
NEG_BIG = -1e30

C_Q, C_K, C_V, C_POOL, C_Z, C_XBC, C_GATE = 0, 512, 1024, 1536, 2048, 2560, 3328
C_DTX = 6400
C_DT8 = 6912
C_END = 7040

TM_PROJ = 512
TM_MERGE = 512
TM_MOE = 512
TME = 256
TA = 256
VMEM_LIMIT = 56 * 1024 * 1024


def _cparams(*sem):
    return pltpu.CompilerParams(dimension_semantics=sem, vmem_limit_bytes=VMEM_LIMIT)


def _rms(x, g):
    return x * lax.rsqrt(jnp.mean(x * x, axis=-1, keepdims=True) + EPS) * g


def _dot(a, b):
    return jnp.dot(a, b, preferred_element_type=F32)


def _dot_nt(a, b):
    return lax.dot_general(a, b, (((1,), (1,)), ((), ())), preferred_element_type=F32)


def _dot_tn(a, b):
    return lax.dot_general(a, b, (((0,), (0,)), ((), ())), preferred_element_type=F32)


def _sigmoid(x):
    return 1.0 / (1.0 + jnp.exp(-x))


def _silu(x):
    return x * _sigmoid(x)


def _softplus(x):
    return jnp.maximum(x, 0.0) + jnp.log(1.0 + jnp.exp(-jnp.abs(x)))


def _shift_rows(v, s, row):
    return jnp.where(row >= s, pltpu.roll(v, s, axis=0), 0.0)


def _chunk_cumsum(v, row_in_chunk):
    s = 1
    while s < CHUNK:
        v = v + jnp.where(row_in_chunk >= s, pltpu.roll(v, s, axis=0), 0.0)
        s *= 2
    return v


def _setup_kernel(table_ref, bucket_ref, vis_ref, lq1_ref, lk1_ref, lq2_ref, lk2_ref,
                  bias_ref, lam_ref, *, lam_inits):
    for h in range(A_HEADS):
        tiles = []
        for d in range(3):
            bkt = bucket_ref[d]
            acc = jnp.zeros(bkt.shape, F32)
            for b in range(REL_BUCKETS):
                acc = acc + jnp.where(bkt == b, table_ref[b, h], 0.0)
            tiles.append(acc)
        bias_ref[h, 0] = jnp.where(vis_ref[...] > 0, tiles[0] - tiles[2], NEG_BIG)
        bias_ref[h, 1] = tiles[1] - tiles[2]
    s1 = jnp.sum(lq1_ref[...] * lk1_ref[...], axis=-1, keepdims=True)
    s2 = jnp.sum(lq2_ref[...] * lk2_ref[...], axis=-1, keepdims=True)
    row = lax.broadcasted_iota(jnp.int32, s1.shape, 0)
    init = jnp.zeros(s1.shape, F32)
    for l, v in enumerate(lam_inits):
        init = jnp.where(row == l, v, init)
    lam_ref[...] = jnp.broadcast_to(jnp.exp(s1) - jnp.exp(s2) + init, lam_ref.shape)


def _rel_bucket_index(rel):
    nb = REL_BUCKETS // 2
    max_exact = nb // 2
    side = jnp.where(rel > 0, nb, 0)
    n = jnp.abs(rel)
    n_f = jnp.maximum(n, 1).astype(F32)
    large = max_exact + (jnp.log(n_f / max_exact) / math.log(REL_MAX_DIST / max_exact)
                         * (nb - max_exact)).astype(jnp.int32)
    large = jnp.minimum(large, nb - 1)
    return side + jnp.where(n < max_exact, n, large)


def _setup(rel_bias_table, lam_q1, lam_k1, lam_q2, lam_k2, lam_inits):
    depth = lam_q1.shape[0]
    r = jnp.arange(TA)
    rel0 = r[None, :] - r[:, None]
    bucket = jnp.stack([_rel_bucket_index(rel0 - d * TA) for d in range(3)]).astype(jnp.int32)
    vis = ((r[None, :] // CHUNK) <= (r[:, None] // CHUNK)).astype(jnp.int32)
    vm = pl.BlockSpec(memory_space=pltpu.VMEM)
    return pl.pallas_call(
        functools.partial(_setup_kernel, lam_inits=lam_inits),
        out_shape=(jax.ShapeDtypeStruct((A_HEADS, 2, TA, TA), F32),
                   jax.ShapeDtypeStruct((depth, LANES), F32)),
        in_specs=[pl.BlockSpec(memory_space=pltpu.SMEM), vm, vm, vm, vm, vm, vm],
        out_specs=(vm, vm),
        name="setup_bias_lam",
    )(rel_bias_table, bucket, vis, lam_q1, lam_k1, lam_q2, lam_k2)


def _inproj_kernel(x_ref, g_ref, w_ref, bg_ref, dtbx_ref, dtb8_ref, alog8_ref,
                   qkv_ref, pool_ref, z_ref, xbc_ref, gate_ref, dtx_ref, cs8_ref):
    h = _rms(x_ref[...], g_ref[...]).astype(BF16)

    def mm(lo, hi):
        return _dot(h, w_ref[:, lo:hi])

    qkv_ref[:, 0:512] = (mm(C_Q, C_K) * (A_QK_DIM ** -0.5)).astype(BF16)
    qkv_ref[:, 512:1024] = mm(C_K, C_V).astype(BF16)
    qkv_ref[:, 1024:1536] = mm(C_V, C_POOL).astype(BF16)
    pool_ref[...] = mm(C_POOL, C_Z).astype(BF16)
    z_ref[...] = mm(C_Z, C_XBC).astype(BF16)
    xbc_ref[:, 0:512] = mm(C_XBC, C_XBC + 512).astype(BF16)
    xbc_ref[:, 512:768] = mm(C_XBC + 512, C_GATE).astype(BF16)
    for j in range(6):
        lo = C_GATE + 512 * j
        gate_ref[:, 512 * j:512 * (j + 1)] = _sigmoid(
            mm(lo, lo + 512) + bg_ref[:, 512 * j:512 * (j + 1)]).astype(BF16)
    dtx_ref[...] = _softplus(mm(C_DTX, C_DT8) + dtbx_ref[...])
    dt8 = _softplus(mm(C_DT8, C_END) + dtb8_ref[...])
    row = lax.broadcasted_iota(jnp.int32, dt8.shape, 0) % CHUNK
    cs8_ref[...] = _chunk_cumsum(dt8 * (-jnp.exp(alog8_ref[...])), row)


def _inproj(x, g, w, bg, dtbx, dtb8, alog8):
    T, D = x.shape
    tm = TM_PROJ
    const = lambda shape: pl.BlockSpec(shape, lambda i: (0, 0), pipeline_mode=pl.Buffered(1))
    row = lambda n: pl.BlockSpec((tm, n), lambda i: (i, 0))
    return pl.pallas_call(
        _inproj_kernel,
        grid=(T // tm,),
        in_specs=[row(D), const((1, D)), const((D, C_END)), const((1, 3072)),
                  const((1, 512)), const((1, LANES)), const((1, LANES))],
        out_specs=(row(1536), row(512), row(512), row(768), row(3072), row(512), row(LANES)),
        out_shape=(jax.ShapeDtypeStruct((T, 1536), BF16), jax.ShapeDtypeStruct((T, 512), BF16),
                   jax.ShapeDtypeStruct((T, 512), BF16), jax.ShapeDtypeStruct((T, 768), BF16),
                   jax.ShapeDtypeStruct((T, 3072), BF16), jax.ShapeDtypeStruct((T, 512), F32),
                   jax.ShapeDtypeStruct((T, LANES), F32)),
        compiler_params=_cparams("parallel"),
        name="inproj",
    )(x, g, w, bg, dtbx, dtb8, alog8)


def _attn_kernel(q_ref, k_ref, v_ref, bias_ref, lam_ref, g_ref, o_ref, *, out_scale):
    S = q_ref.shape[0]
    lam = lam_ref[...]
    g = g_ref[...]
    for i in range(S // TA):
        q = q_ref[i * TA:(i + 1) * TA, :]
        lane = lax.broadcasted_iota(jnp.int32, q.shape, 1)
        qs = (jnp.where(lane < A_QK_DIM, q, jnp.zeros_like(q)),
              jnp.where(lane >= A_QK_DIM, q, jnp.zeros_like(q)))
        pieces = []
        if i >= 2:
            pieces.append((0, (i - 1) * TA, None))
        if i >= 1:
            pieces.append(((i - 1) * TA, i * TA, 1))
        pieces.append((i * TA, (i + 1) * TA, 0))
        outs = []
        for qm in qs:
            ss = []
            for lo, hi, bi in pieces:
                s = _dot_nt(qm, k_ref[lo:hi, :])
                if bi is not None:
                    s = s + bias_ref[bi]
                ss.append(s)
            m = jnp.max(ss[0], axis=-1, keepdims=True)
            for s in ss[1:]:
                m = jnp.maximum(m, jnp.max(s, axis=-1, keepdims=True))
            l = jnp.zeros((TA, 1), F32)
            acc = jnp.zeros((TA, A_V_DIM), F32)
            for s, (lo, hi, _) in zip(ss, pieces):
                p = jnp.exp(s - m)
                l = l + jnp.sum(p, axis=-1, keepdims=True)
                acc = acc + _dot(p.astype(BF16), v_ref[lo:hi, :])
            outs.append(acc / l)
        o = outs[0] - lam * outs[1]
        o_ref[i * TA:(i + 1) * TA, :] = (_rms(o, g) * out_scale).astype(o_ref.dtype)


def _attention(qkv, bias, lam_row, subln_g, out_scale, B, S):
    qkv3 = qkv.reshape(B, S, 3 * A_WIDTH)
    seq = lambda off: pl.BlockSpec((None, S, LANES), lambda b, h: (b, 0, off + h))
    return pl.pallas_call(
        functools.partial(_attn_kernel, out_scale=out_scale),
        grid=(B, A_HEADS),
        in_specs=[seq(0), seq(A_HEADS), seq(2 * A_HEADS),
                  pl.BlockSpec((None, 2, TA, TA), lambda b, h: (h, 0, 0, 0)),
                  pl.BlockSpec((1, LANES), lambda b, h: (0, 0)),
                  pl.BlockSpec((1, LANES), lambda b, h: (0, 0))],
        out_specs=seq(0),
        out_shape=jax.ShapeDtypeStruct((B, S, A_WIDTH), BF16),
        compiler_params=_cparams("parallel", "parallel"),
        name="diff_attention",
    )(qkv3, qkv3, qkv3, bias, lam_row, subln_g)


def _pool_kernel(u_ref, w_ref, scale_ref, o_ref):
    S = u_ref.shape[0]
    row = lax.broadcasted_iota(jnp.int32, (S, POOL_GROUP_DIM), 0)
    t = (row + 1).astype(F32)
    for gi, win in enumerate(POOL_WINDOWS):
        sl = slice(gi * POOL_GROUP_DIM, (gi + 1) * POOL_GROUP_DIM)
        u = u_ref[:, sl].astype(F32)
        acc = u
        s = 1
        while s < win:
            acc = acc + _shift_rows(acc, s, row)
            s *= 2
        d = acc / jnp.minimum(t, float(win)) - u
        y = _dot(d.astype(BF16), w_ref[gi])
        o_ref[:, sl] = (y * scale_ref[:, sl]).astype(o_ref.dtype)


def _pool(u, w_group, scale, B, S):
    u3 = u.reshape(B, S, POOL_WIDTH)
    return pl.pallas_call(
        _pool_kernel,
        grid=(B,),
        in_specs=[pl.BlockSpec((None, S, POOL_WIDTH), lambda b: (b, 0, 0)),
                  pl.BlockSpec((len(POOL_WINDOWS), POOL_GROUP_DIM, POOL_GROUP_DIM),
                               lambda b: (0, 0, 0)),
                  pl.BlockSpec((1, POOL_WIDTH), lambda b: (0, 0))],
        out_specs=pl.BlockSpec((None, S, POOL_WIDTH), lambda b: (b, 0, 0)),
        out_shape=jax.ShapeDtypeStruct((B, S, POOL_WIDTH), BF16),
        compiler_params=_cparams("parallel"),
        name="pool_mixer",
    )(u3, w_group, scale)


def _ssd_kernel(xbc_ref, z_ref, dtx_ref, r_ref, convw_ref, convb_ref, alogx_ref, dx_ref, g_ref,
                o_ref, conv_ref, csx_ref, state_ref):
    S = xbc_ref.shape[0]
    nc = S // CHUNK
    GN = SSD_GROUPS * SSD_STATE
    HP = SSD_WIDTH

    xin = xbc_ref[...].astype(F32)
    row = lax.broadcasted_iota(jnp.int32, xin.shape, 0)
    acc = xin * convw_ref[SSD_CONV - 1:SSD_CONV, :] + convb_ref[...]
    for s in range(1, SSD_CONV):
        acc = acc + _shift_rows(xin, s, row) * convw_ref[SSD_CONV - 1 - s:SSD_CONV - s, :]
    conv_ref[...] = _silu(acc)

    row_c = lax.broadcasted_iota(jnp.int32, (S, HP), 0) % CHUNK
    csx_ref[...] = _chunk_cumsum(dtx_ref[...] * (-jnp.exp(alogx_ref[...])), row_c)

    state_ref[...] = jnp.zeros(state_ref.shape, F32)

    li = lax.broadcasted_iota(jnp.int32, (CHUNK, HP), 0)
    ci = lax.broadcasted_iota(jnp.int32, (CHUNK, HP), 1)
    tril = li >= (ci % CHUNK)
    br = lax.broadcasted_iota(jnp.int32, (HP, GN), 0)
    bc = lax.broadcasted_iota(jnp.int32, (HP, GN), 1)
    bmask = (br // (HP // SSD_GROUPS)) == (bc // SSD_STATE)
    sr = lax.broadcasted_iota(jnp.int32, (GN, HP), 0)
    sc = lax.broadcasted_iota(jnp.int32, (GN, HP), 1)
    smask = (sr // SSD_STATE) == (sc // (HP // SSD_GROUPS))
    GW = HP // SSD_GROUPS
    dr = lax.broadcasted_iota(jnp.int32, (GW, GW), 0)
    dc = lax.broadcasted_iota(jnp.int32, (GW, GW), 1)
    dmask = (dr // CHUNK) == (dc // SSD_HEAD_DIM)

    def chunk_body(c, carry):
        r0 = pl.multiple_of(c * CHUNK, CHUNK)
        rows = pl.ds(r0, CHUNK)
        xs = conv_ref[rows, 0:HP]
        Bc = conv_ref[rows, HP:HP + GN].astype(BF16)
        Cc = conv_ref[rows, HP + GN:HP + 2 * GN].astype(BF16)
        dtx = dtx_ref[rows, :]
        cs = csx_ref[rows, :]
        cs_row = r_ref[pl.ds(c, 1), :]
        cs_end = csx_ref[pl.ds(r0 + CHUNK - 1, 1), :]

        xdt = xs * dtx
        Lm = jnp.exp(jnp.where(tril, cs - cs_row, NEG_BIG))
        b_rep = jnp.where(bmask, jnp.concatenate([Bc] * SSD_HEADS, axis=0), jnp.zeros((), BF16))
        scores = _dot_nt(Cc, b_rep)
        G = (scores * Lm).astype(BF16)

        xdt_b = xdt.astype(BF16)
        y_parts = []
        for g in range(SSD_GROUPS):
            xg = xdt_b[:, g * GW:(g + 1) * GW]
            xbd = jnp.where(dmask, jnp.concatenate([xg] * (GW // CHUNK), axis=0),
                            jnp.zeros((), BF16))
            y_parts.append(_dot(G[:, g * GW:(g + 1) * GW], xbd))
        y = jnp.concatenate(y_parts, axis=1)

        st = state_ref[...]
        y = y + _dot(Cc, st.astype(BF16)) * jnp.exp(cs)
        xdec = (xdt * jnp.exp(cs_end - cs)).astype(BF16)
        upd = jnp.where(smask, _dot_tn(Bc, xdec), 0.0)
        state_ref[...] = st * jnp.exp(cs_end) + upd

        y = y + dx_ref[...] * xs
        y = y * _silu(z_ref[rows, :].astype(F32))
        o_ref[rows, :] = _rms(y, g_ref[...]).astype(o_ref.dtype)
        return carry

    lax.fori_loop(0, nc, chunk_body, 0)


def _ssd(xbc, z, dtx, r, conv_w, conv_b, alogx, dx, norm_g, B, S):
    nc = S // CHUNK
    b3 = lambda n: pl.BlockSpec((None, S, n), lambda b: (b, 0, 0))
    const = lambda shape: pl.BlockSpec(shape, lambda b: (0,) * len(shape))
    return pl.pallas_call(
        _ssd_kernel,
        grid=(B,),
        in_specs=[b3(SSD_CONV_CH), b3(SSD_WIDTH), b3(SSD_WIDTH),
                  pl.BlockSpec((None, nc, SSD_WIDTH), lambda b: (b, 0, 0)),
                  const((SSD_CONV, SSD_CONV_CH)), const((1, SSD_CONV_CH)),
                  const((1, SSD_WIDTH)), const((1, SSD_WIDTH)), const((1, SSD_WIDTH))],
        out_specs=b3(SSD_WIDTH),
        out_shape=jax.ShapeDtypeStruct((B, S, SSD_WIDTH), BF16),
        scratch_shapes=[pltpu.VMEM((S, SSD_CONV_CH), F32), pltpu.VMEM((S, SSD_WIDTH), F32),
                        pltpu.VMEM((SSD_GROUPS * SSD_STATE, SSD_WIDTH), F32)],
        compiler_params=_cparams("parallel"),
        name="ssd_mixer",
    )(xbc.reshape(B, S, SSD_CONV_CH), z.reshape(B, S, SSD_WIDTH), dtx.reshape(B, S, SSD_WIDTH),
      r, conv_w, conv_b, alogx, dx, norm_g)


def _memkv_kernel(mem_ref, g_ref, w_ref, kv_ref):
    kv_ref[...] = _dot(_rms(mem_ref[...], g_ref[...]).astype(BF16), w_ref[...]).astype(kv_ref.dtype)


def _memkv(mem, g_mem, w_xkv):
    B, M, D = mem.shape
    return pl.pallas_call(
        _memkv_kernel,
        grid=(B,),
        in_specs=[pl.BlockSpec((None, M, D), lambda b: (b, 0, 0)),
                  pl.BlockSpec((1, D), lambda b: (0, 0)),
                  pl.BlockSpec((D, 2 * X_WIDTH), lambda b: (0, 0))],
        out_specs=pl.BlockSpec((None, M, 2 * X_WIDTH), lambda b: (b, 0, 0)),
        out_shape=jax.ShapeDtypeStruct((B, M, 2 * X_WIDTH), BF16),
        compiler_params=_cparams("parallel"),
        name="mem_kv",
    )(mem, g_mem, w_xkv)


def _route(lg, le):
    lane = lax.broadcasted_iota(jnp.int32, lg.shape, 1)
    big = jnp.int32(LANES)
    lgm = jnp.where(lane < N_EGROUPS, lg, NEG_BIG)
    gmax = jnp.max(lgm, axis=-1, keepdims=True)
    g_p = 1.0 / jnp.sum(jnp.exp(lgm - gmax), axis=-1, keepdims=True)
    g_idx = jnp.min(jnp.where(lgm == gmax, lane, big), axis=-1, keepdims=True)
    in_grp = (lane // EXPERTS_PER_GROUP == g_idx) & (lane < N_EXPERTS)
    e1 = jnp.where(in_grp, le, NEG_BIG)
    v1 = jnp.max(e1, axis=-1, keepdims=True)
    i1 = jnp.min(jnp.where(e1 == v1, lane, big), axis=-1, keepdims=True)
    e2 = jnp.where(lane == i1, NEG_BIG, e1)
    v2 = jnp.max(e2, axis=-1, keepdims=True)
    i2 = jnp.min(jnp.where(e2 == v2, lane, big), axis=-1, keepdims=True)
    t = jnp.exp(v2 - v1)
    w1 = g_p / (1.0 + t)
    w2 = g_p * t / (1.0 + t)
    return lane, i1, i2, w1, w2


def _merge_kernel(x_ref, oa_ref, ob_ref, oc_ref, gate_ref, wb_ref, wmix_ref, gx_ref, wq_ref,
                  kv_ref, wo_ref, gm_ref, wr_ref, br_ref, x2_ref, hp_ref, route_ref, cnt_ref):
    D = x_ref.shape[1]
    merged = gate_ref[:, 0:D].astype(F32) * _dot(oa_ref[...], wb_ref[0])
    merged = merged + gate_ref[:, D:2 * D].astype(F32) * _dot(ob_ref[...], wb_ref[1])
    merged = merged + gate_ref[:, 2 * D:3 * D].astype(F32) * _dot(oc_ref[...], wb_ref[2])
    x1 = x_ref[...] + _dot(merged.astype(BF16), wmix_ref[...])

    hq = _rms(x1, gx_ref[...]).astype(BF16)
    q = (_dot(hq, wq_ref[...]) * (X_HEAD_DIM ** -0.5)).astype(BF16)
    k = kv_ref[:, 0:X_WIDTH]
    v = kv_ref[:, X_WIDTH:2 * X_WIDTH]
    qlane = lax.broadcasted_iota(jnp.int32, q.shape, 1) // X_HEAD_DIM
    vlane = lax.broadcasted_iota(jnp.int32, v.shape, 1) // X_HEAD_DIM
    o = jnp.zeros(q.shape, F32)
    for hd in range(X_HEADS):
        s = _dot_nt(jnp.where(qlane == hd, q, jnp.zeros_like(q)), k)
        p = jnp.exp(s - jnp.max(s, axis=-1, keepdims=True))
        p = p / jnp.sum(p, axis=-1, keepdims=True)
        o = o + _dot(p.astype(BF16), jnp.where(vlane == hd, v, jnp.zeros_like(v)))
    x2 = x1 + _dot(o.astype(BF16), wo_ref[...])
    x2_ref[...] = x2

    hm = _rms(x2, gm_ref[...]).astype(BF16)
    bits = lax.bitcast_convert_type(hm.astype(F32), jnp.uint32)
    hp_ref[...] = (bits[:, 0:D // 2] >> 16) | (bits[:, D // 2:D] & jnp.uint32(0xFFFF0000))
    logits = _dot(hm, wr_ref[...]) + br_ref[...]
    lane, i1, i2, w1, w2 = _route(logits[:, 0:LANES], logits[:, LANES:2 * LANES])

    @pl.when(pl.program_id(0) == 0)
    def _():
        cnt_ref[...] = jnp.zeros(cnt_ref.shape, F32)

    oh1 = (lane == i1).astype(F32)
    oh2 = (lane == i2).astype(F32)
    tm = oh1.shape[0]
    tr = lax.broadcasted_iota(jnp.int32, (tm, tm), 0)
    tc = lax.broadcasted_iota(jnp.int32, (tm, tm), 1)
    before = jnp.where(tr > tc, 1.0, 0.0).astype(BF16)
    prior = _dot(before, (oh1 + oh2).astype(BF16)) + cnt_ref[0:1, :]
    rank1 = jnp.sum(prior * oh1, axis=-1, keepdims=True)
    rank2 = jnp.sum(prior * oh2, axis=-1, keepdims=True)
    cnt_ref[...] = cnt_ref[...] + jnp.sum(oh1 + oh2, axis=0, keepdims=True)
    route = jnp.where(lane == 0, i1.astype(F32), 0.0)
    route = jnp.where(lane == 1, i2.astype(F32), route)
    route = jnp.where(lane == 2, w1, route)
    route = jnp.where(lane == 3, w2, route)
    route = jnp.where(lane == 4, rank1, route)
    route_ref[...] = jnp.where(lane == 5, rank2, route)


def _merge(x, oa, ob, oc, gates, wb, wmix, gx, wq, kv, wo, gm, wr, br, S):
    T, D = x.shape
    tm = TM_MERGE
    per_b = S // tm
    row = lambda n: pl.BlockSpec((tm, n), lambda i: (i, 0))
    const = lambda shape: pl.BlockSpec(shape, lambda i: (0,) * len(shape),
                                       pipeline_mode=pl.Buffered(1))
    return pl.pallas_call(
        _merge_kernel,
        grid=(T // tm,),
        in_specs=[row(D), row(512), row(512), row(512), row(3 * D),
                  const((N_BRANCH, 512, D)), const((D, D)), const((1, D)), const((D, X_WIDTH)),
                  pl.BlockSpec((None, N_MEM, 2 * X_WIDTH), lambda i: (i // per_b, 0, 0)),
                  const((X_WIDTH, D)), const((1, D)), const((D, 2 * LANES)), const((1, 2 * LANES))],
        out_specs=(row(D), row(D // 2), row(LANES), pl.BlockSpec((8, LANES), lambda i: (0, 0))),
        out_shape=(jax.ShapeDtypeStruct((T, D), F32), jax.ShapeDtypeStruct((T, D // 2), jnp.uint32),
                   jax.ShapeDtypeStruct((T, LANES), F32), jax.ShapeDtypeStruct((8, LANES), F32)),
        compiler_params=_cparams("arbitrary"),
        name="merge_xattn_router",
    )(x, oa, ob, oc, gates, wb, wmix, gx, wq, kv, wo, gm, wr, br)


def _plan(route, cnt, T):
    e = route[:, 0:2].astype(jnp.int32)
    rank = route[:, 4:6].astype(jnp.int32)
    counts = cnt[0, :N_EXPERTS].astype(jnp.int32)
    padded = ((counts + TME - 1) // TME) * TME
    ends = jnp.cumsum(padded)
    off = ends - padded
    onehot = e[..., None] == jnp.arange(N_EXPERTS, dtype=jnp.int32)
    slots = (jnp.sum(jnp.where(onehot, off, 0), axis=-1) + rank).reshape(2 * T)
    n_tiles = (2 * T) // TME + N_EXPERTS
    tile_start = jnp.arange(n_tiles, dtype=jnp.int32) * TME
    tile_expert = jnp.minimum(jnp.sum(tile_start[:, None] >= ends[None, :], axis=1),
                              N_EXPERTS - 1).astype(jnp.int32)
    n_used = ends[-1] // TME
    tile_src = jnp.where(jnp.arange(n_tiles) < n_used, jnp.arange(n_tiles), 0).astype(jnp.int32)
    pad_tile = jnp.where(padded > 0, ends - TME, -1).astype(jnp.int32)
    return slots, tile_expert, tile_src, n_used.reshape(1).astype(jnp.int32), pad_tile


def _row_copy(src, dst, i, j, sem):
    return pltpu.make_async_copy(src.at[pl.ds(i, 1)], dst.at[pl.ds(j, 1)], sem)


def _scatter_kernel(slots_ref, pad_ref, nu_ref, hp_ref, hs_ref, zbuf_ref, sem, zsem):
    tm = hp_ref.shape[0]

    def zero_tile(row):
        start = pl.multiple_of(row, TME)
        return pltpu.make_async_copy(zbuf_ref, hs_ref.at[pl.ds(start, TME)], zsem)

    @pl.when(pl.program_id(0) == 0)
    def _():
        zbuf_ref[...] = jnp.zeros(zbuf_ref.shape, zbuf_ref.dtype)
        for e in range(N_EXPERTS):
            @pl.when(pad_ref[e] >= 0)
            def _():
                zero_tile(pad_ref[e]).start()
        for e in range(N_EXPERTS):
            @pl.when(pad_ref[e] >= 0)
            def _():
                zero_tile(pad_ref[e]).wait()

        def tail(t, c):
            zero_tile(t * TME).start()
            zero_tile(t * TME).wait()
            return c

        lax.fori_loop(nu_ref[0], hs_ref.shape[0] // TME, tail, 0)

    def start(r, c):
        _row_copy(hp_ref, hs_ref, r, slots_ref[2 * r], sem).start()
        _row_copy(hp_ref, hs_ref, r, slots_ref[2 * r + 1], sem).start()
        return c

    def wait(r, c):
        _row_copy(hp_ref, hs_ref, r, slots_ref[2 * r], sem).wait()
        _row_copy(hp_ref, hs_ref, r, slots_ref[2 * r + 1], sem).wait()
        return c

    lax.fori_loop(0, tm, start, 0, unroll=8)
    lax.fori_loop(0, tm, wait, 0, unroll=8)


def _scatter(slots, pad_tile, n_used, hp, n_rows):
    T, W = hp.shape
    tm = TM_MOE
    return pl.pallas_call(
        _scatter_kernel,
        grid=(T // tm,),
        in_specs=[pl.BlockSpec((2 * tm,), lambda i: (i,), memory_space=pltpu.SMEM),
                  pl.BlockSpec(memory_space=pltpu.SMEM),
                  pl.BlockSpec(memory_space=pltpu.SMEM),
                  pl.BlockSpec((tm, W), lambda i: (i, 0))],
        out_specs=pl.BlockSpec(memory_space=pl.ANY),
        out_shape=jax.ShapeDtypeStruct((n_rows, W), hp.dtype),
        scratch_shapes=[pltpu.VMEM((TME, W), hp.dtype), pltpu.SemaphoreType.DMA(()),
                        pltpu.SemaphoreType.DMA(())],
        compiler_params=_cparams("arbitrary"),
        name="moe_scatter",
    )(slots, pad_tile, n_used, hp)


def _expert_kernel(te_ref, ts_ref, nu_ref, hs_ref, wg_ref, wu_ref, wd_ref, ys_ref):
    i = pl.program_id(0)

    @pl.when(i < nu_ref[0])
    def _():
        u = hs_ref[...]
        half = wg_ref.shape[0] // 2
        lo = lax.bitcast_convert_type(u << 16, F32).astype(BF16)
        hi = lax.bitcast_convert_type(u & jnp.uint32(0xFFFF0000), F32).astype(BF16)
        gate = _dot(lo, wg_ref[0:half, :]) + _dot(hi, wg_ref[half:, :])
        up = _dot(lo, wu_ref[0:half, :]) + _dot(hi, wu_ref[half:, :])
        ys_ref[...] = _dot((_silu(gate) * up).astype(BF16), wd_ref[...])

    @pl.when(i >= nu_ref[0])
    def _():
        ys_ref[...] = jnp.zeros(ys_ref.shape, ys_ref.dtype)


def _experts(tile_expert, tile_src, n_used, hs, wg, wu, wd):
    P, W = hs.shape
    D = 2 * W
    wspec = lambda a, b: pl.BlockSpec((None, a, b), lambda i, te, ts, nu: (te[i], 0, 0))
    grid_spec = pltpu.PrefetchScalarGridSpec(
        num_scalar_prefetch=3,
        grid=(P // TME,),
        in_specs=[pl.BlockSpec((TME, W), lambda i, te, ts, nu: (ts[i], 0)),
                  wspec(D, EXPERT_FF), wspec(D, EXPERT_FF), wspec(EXPERT_FF, D)],
        out_specs=pl.BlockSpec((TME, D), lambda i, te, ts, nu: (i, 0)))
    return pl.pallas_call(
        _expert_kernel,
        grid_spec=grid_spec,
        out_shape=jax.ShapeDtypeStruct((P, D), F32),
        compiler_params=_cparams("arbitrary"),
        name="moe_experts",
    )(tile_expert, tile_src, n_used, hs, wg, wu, wd)


def _combine_kernel(slots_ref, x_ref, route_ref, ys_ref, gf_ref, o_ref, ybuf_ref, sem, *, final):
    tm = x_ref.shape[0]

    def copy(r, k):
        return pltpu.make_async_copy(ys_ref.at[pl.ds(slots_ref[2 * r + k], 1)],
                                     ybuf_ref.at[k, pl.ds(r, 1)], sem)

    def start(r, c):
        copy(r, 0).start()
        copy(r, 1).start()
        return c

    def wait(r, c):
        copy(r, 0).wait()
        copy(r, 1).wait()
        return c

    lax.fori_loop(0, tm, start, 0, unroll=8)
    lax.fori_loop(0, tm, wait, 0, unroll=8)
    route = route_ref[...]
    y = x_ref[...] + route[:, 2:3] * ybuf_ref[0] + route[:, 3:4] * ybuf_ref[1]
    o_ref[...] = _rms(y, gf_ref[...]) if final else y


def _combine(slots, x2, route, ys, g_final, final):
    T, D = x2.shape
    tm = TM_MOE
    row = lambda n: pl.BlockSpec((tm, n), lambda i: (i, 0))
    return pl.pallas_call(
        functools.partial(_combine_kernel, final=final),
        grid=(T // tm,),
        in_specs=[pl.BlockSpec((2 * tm,), lambda i: (i,), memory_space=pltpu.SMEM),
                  row(D), row(LANES), pl.BlockSpec(memory_space=pl.ANY),
                  pl.BlockSpec((1, D), lambda i: (0, 0))],
        out_specs=row(D),
        out_shape=jax.ShapeDtypeStruct((T, D), F32),
        scratch_shapes=[pltpu.VMEM((2, tm, D), F32), pltpu.SemaphoreType.DMA(())],
        compiler_params=_cparams("arbitrary"),
        name="moe_combine",
    )(slots, x2, route, ys, g_final)


def _prep_inproj_weight(w_in_l):
    col_dt = 2560 + SSD_CONV_CH
    w_dt = w_in_l[:, col_dt:col_dt + SSD_HEADS]
    return jnp.concatenate(
        [w_in_l[:, :col_dt], w_in_l[:, col_dt + SSD_HEADS:],
         jnp.repeat(w_dt, SSD_HEAD_DIM, axis=1),
         jnp.pad(w_dt, ((0, 0), (0, LANES - SSD_HEADS)))], axis=1).astype(BF16)


def _pad_lanes(v, n=LANES):
    return jnp.pad(v, (0, n - v.shape[0]))[None, :]


def kernel(x, mem, w_in, b_gate, lam_q1, lam_k1, lam_q2, lam_k2, attn_subln_g, rel_bias_table,
           w_pool_group, pool_scale, ssd_conv_w, ssd_conv_b, ssd_dt_bias, ssd_A_log, ssd_D,
           ssd_norm_g, w_branch, w_mix_out, g_mix, g_xattn, g_mem, w_xq, w_xkv, w_xo, g_moe,
           w_router_group, b_router_group, w_router_expert, b_router_expert,
           w_exp_gate, w_exp_up, w_exp_down, g_final):
    B, S, D = x.shape
    T = B * S
    depth = w_in.shape[0]
    nc = S // CHUNK
    lam_inits = tuple(0.8 - 0.6 * math.exp(-0.3 * l) for l in range(depth))
    bias, lam = _setup(rel_bias_table, lam_q1, lam_k1, lam_q2, lam_k2, lam_inits)

    xf = x.reshape(T, D)
    for l in range(depth):
        w = _prep_inproj_weight(w_in[l])
        qkv, pool_u, z, xbc, gates, dtx, cs8 = _inproj(
            xf, g_mix[l][None, :], w, b_gate[l][None, :],
            jnp.repeat(ssd_dt_bias[l], SSD_HEAD_DIM)[None, :], _pad_lanes(ssd_dt_bias[l]),
            _pad_lanes(ssd_A_log[l]))
        o_a = _attention(qkv, bias, lam[l][None, :], attn_subln_g[l][None, :],
                         1.0 - lam_inits[l], B, S)
        o_b = _pool(pool_u, w_pool_group[l].astype(BF16), pool_scale[l][None, :], B, S)
        r = cs8[:, :SSD_HEADS].reshape(B, nc, CHUNK, SSD_HEADS).transpose(0, 1, 3, 2)
        r = r.reshape(B, nc, SSD_WIDTH)
        o_c = _ssd(xbc, z, dtx, r, ssd_conv_w[l], ssd_conv_b[l][None, :],
                   jnp.repeat(ssd_A_log[l], SSD_HEAD_DIM)[None, :],
                   jnp.repeat(ssd_D[l], SSD_HEAD_DIM)[None, :], ssd_norm_g[l][None, :], B, S)
        kv = _memkv(mem, g_mem[l][None, :], w_xkv[l].astype(BF16))
        w_r = jnp.concatenate(
            [jnp.pad(w_router_group[l], ((0, 0), (0, LANES - N_EGROUPS))),
             jnp.pad(w_router_expert[l], ((0, 0), (0, LANES - N_EXPERTS)))], axis=1).astype(BF16)
        b_r = jnp.concatenate([_pad_lanes(b_router_group[l]), _pad_lanes(b_router_expert[l])],
                              axis=1)
        x2, hp, route, cnt = _merge(
            xf, o_a.reshape(T, A_WIDTH), o_b.reshape(T, POOL_WIDTH), o_c.reshape(T, SSD_WIDTH),
            gates, w_branch[l].astype(BF16), w_mix_out[l].astype(BF16), g_xattn[l][None, :],
            w_xq[l].astype(BF16), kv, w_xo[l].astype(BF16), g_moe[l][None, :], w_r, b_r, S)
        slots, tile_expert, tile_src, n_used, pad_tile = _plan(route, cnt, T)
        hs = _scatter(slots, pad_tile, n_used, hp, 2 * T + N_EXPERTS * TME)
        ys = _experts(tile_expert, tile_src, n_used, hs, w_exp_gate[l].astype(BF16),
                      w_exp_up[l].astype(BF16), w_exp_down[l].astype(BF16))
        xf = _combine(slots, x2, route, ys, g_final[None, :], l == depth - 1)
    return xf.reshape(B, S, D)
```

```python
import functools
import math

import jax
import jax.numpy as jnp
from jax import lax
from jax.experimental import pallas as pl
from jax.experimental.pallas import tpu as pltpu

F32 = jnp.float32
BF16 = jnp.bfloat16

EPS = 1e-6
CHUNK = 64
N_MEM = 256

A_HEADS = 4
A_QK_DIM = 64
A_V_DIM = 128
A_WIDTH = 512
REL_BUCKETS = 32
REL_MAX_DIST = 128

POOL_WINDOWS = (2, 4, 8, 16)
POOL_GROUP_DIM = 128
POOL_WIDTH = 512

SSD_HEADS = 8
SSD_HEAD_DIM = 64
SSD_WIDTH = 512
SSD_GROUPS = 2
SSD_STATE = 64
SSD_CONV = 4
SSD_CONV_CH = 768

N_BRANCH = 3
X_HEADS = 4
X_HEAD_DIM = 64
X_WIDTH = 256

N_EGROUPS = 4
EXPERTS_PER_GROUP = 4
N_EXPERTS = 16
EXPERT_FF = 512

LANES = 128
NEG_BIG = -1e30

C_Q, C_K, C_V, C_POOL, C_Z, C_XBC, C_GATE = 0, 512, 1024, 1536, 2048, 2560, 3328
C_DTX = 6400
C_DT8 = 6912
C_END = 7040

TM_PROJ = 512
TM_MERGE = 512
TM_MOE = 512
TME = 256
TA = 256
VMEM_LIMIT = 56 * 1024 * 1024


def _cparams(*sem):
    return pltpu.CompilerParams(dimension_semantics=sem, vmem_limit_bytes=VMEM_LIMIT)


def _rms(x, g):
    return x * lax.rsqrt(jnp.mean(x * x, axis=-1, keepdims=True) + EPS) * g


def _dot(a, b):
    return jnp.dot(a, b, preferred_element_type=F32)


def _dot_nt(a, b):
    return lax.dot_general(a, b, (((1,), (1,)), ((), ())), preferred_element_type=F32)


def _dot_tn(a, b):
    return lax.dot_general(a, b, (((0,), (0,)), ((), ())), preferred_element_type=F32)


def _sigmoid(x):
    return 1.0 / (1.0 + jnp.exp(-x))


def _silu(x):
    return x * _sigmoid(x)


def _softplus(x):
    return jnp.maximum(x, 0.0) + jnp.log(1.0 + jnp.exp(-jnp.abs(x)))


ROW_SUB = 8


def _store_rows(ref, x):
    m = x.shape[0]
    for s in range(ROW_SUB):
        ref[pl.ds(s, m, stride=ROW_SUB), :] = x[:, s * LANES:(s + 1) * LANES]


def _load_rows(ref):
    m = ref.shape[0] // ROW_SUB
    return jnp.concatenate([ref[pl.ds(s, m, stride=ROW_SUB), :] for s in range(ROW_SUB)], axis=1)


def _row_tile(ref, r):
    return ref.at[pl.ds(pl.multiple_of(r * ROW_SUB, ROW_SUB), ROW_SUB)]


def _shift_rows(v, s, row):
    return jnp.where(row >= s, pltpu.roll(v, s, axis=0), 0.0)


def _chunk_cumsum(v, row_in_chunk):
    s = 1
    while s < CHUNK:
        v = v + jnp.where(row_in_chunk >= s, pltpu.roll(v, s, axis=0), 0.0)
        s *= 2
    return v


def _setup_kernel(table_ref, bucket_ref, vis_ref, lq1_ref, lk1_ref, lq2_ref, lk2_ref,
                  bias_ref, lam_ref, *, lam_inits):
    for h in range(A_HEADS):
        tiles = []
        for d in range(3):
            bkt = bucket_ref[d]
            acc = jnp.zeros(bkt.shape, F32)
            for b in range(REL_BUCKETS):
                acc = acc + jnp.where(bkt == b, table_ref[b, h], 0.0)
            tiles.append(acc)
        bias_ref[h, 0] = jnp.where(vis_ref[...] > 0, tiles[0] - tiles[2], NEG_BIG)
        bias_ref[h, 1] = tiles[1] - tiles[2]
    s1 = jnp.sum(lq1_ref[...] * lk1_ref[...], axis=-1, keepdims=True)
    s2 = jnp.sum(lq2_ref[...] * lk2_ref[...], axis=-1, keepdims=True)
    row = lax.broadcasted_iota(jnp.int32, s1.shape, 0)
    init = jnp.zeros(s1.shape, F32)
    for l, v in enumerate(lam_inits):
        init = jnp.where(row == l, v, init)
    lam_ref[...] = jnp.broadcast_to(jnp.exp(s1) - jnp.exp(s2) + init, lam_ref.shape)


def _rel_bucket_index(rel):
    nb = REL_BUCKETS // 2
    max_exact = nb // 2
    side = jnp.where(rel > 0, nb, 0)
    n = jnp.abs(rel)
    n_f = jnp.maximum(n, 1).astype(F32)
    large = max_exact + (jnp.log(n_f / max_exact) / math.log(REL_MAX_DIST / max_exact)
                         * (nb - max_exact)).astype(jnp.int32)
    large = jnp.minimum(large, nb - 1)
    return side + jnp.where(n < max_exact, n, large)


def _setup(rel_bias_table, lam_q1, lam_k1, lam_q2, lam_k2, lam_inits):
    depth = lam_q1.shape[0]
    r = jnp.arange(TA)
    rel0 = r[None, :] - r[:, None]
    bucket = jnp.stack([_rel_bucket_index(rel0 - d * TA) for d in range(3)]).astype(jnp.int32)
    vis = ((r[None, :] // CHUNK) <= (r[:, None] // CHUNK)).astype(jnp.int32)
    vm = pl.BlockSpec(memory_space=pltpu.VMEM)
    return pl.pallas_call(
        functools.partial(_setup_kernel, lam_inits=lam_inits),
        out_shape=(jax.ShapeDtypeStruct((A_HEADS, 2, TA, TA), F32),
                   jax.ShapeDtypeStruct((depth, LANES), F32)),
        in_specs=[pl.BlockSpec(memory_space=pltpu.SMEM), vm, vm, vm, vm, vm, vm],
        out_specs=(vm, vm),
        name="setup_bias_lam",
    )(rel_bias_table, bucket, vis, lam_q1, lam_k1, lam_q2, lam_k2)


def _inproj_kernel(x_ref, g_ref, w_ref, bg_ref, dtbx_ref, dtb8_ref, alog8_ref,
                   qkv_ref, pool_ref, z_ref, xbc_ref, gate_ref, dtx_ref, cs8_ref):
    h = _rms(x_ref[...], g_ref[...]).astype(BF16)

    def mm(lo, hi):
        return _dot(h, w_ref[:, lo:hi])

    qkv_ref[:, 0:512] = (mm(C_Q, C_K) * (A_QK_DIM ** -0.5)).astype(BF16)
    qkv_ref[:, 512:1024] = mm(C_K, C_V).astype(BF16)
    qkv_ref[:, 1024:1536] = mm(C_V, C_POOL).astype(BF16)
    pool_ref[...] = mm(C_POOL, C_Z).astype(BF16)
    z_ref[...] = mm(C_Z, C_XBC).astype(BF16)
    xbc_ref[:, 0:512] = mm(C_XBC, C_XBC + 512).astype(BF16)
    xbc_ref[:, 512:768] = mm(C_XBC + 512, C_GATE).astype(BF16)
    for j in range(6):
        lo = C_GATE + 512 * j
        gate_ref[:, 512 * j:512 * (j + 1)] = _sigmoid(
            mm(lo, lo + 512) + bg_ref[:, 512 * j:512 * (j + 1)]).astype(BF16)
    dtx_ref[...] = _softplus(mm(C_DTX, C_DT8) + dtbx_ref[...])
    dt8 = _softplus(mm(C_DT8, C_END) + dtb8_ref[...])
    row = lax.broadcasted_iota(jnp.int32, dt8.shape, 0) % CHUNK
    cs8_ref[...] = _chunk_cumsum(dt8 * (-jnp.exp(alog8_ref[...])), row)


def _inproj(x, g, w, bg, dtbx, dtb8, alog8):
    T, D = x.shape
    tm = TM_PROJ
    const = lambda shape: pl.BlockSpec(shape, lambda i: (0, 0), pipeline_mode=pl.Buffered(1))
    row = lambda n: pl.BlockSpec((tm, n), lambda i: (i, 0))
    return pl.pallas_call(
        _inproj_kernel,
        grid=(T // tm,),
        in_specs=[row(D), const((1, D)), const((D, C_END)), const((1, 3072)),
                  const((1, 512)), const((1, LANES)), const((1, LANES))],
        out_specs=(row(1536), row(512), row(512), row(768), row(3072), row(512), row(LANES)),
        out_shape=(jax.ShapeDtypeStruct((T, 1536), BF16), jax.ShapeDtypeStruct((T, 512), BF16),
                   jax.ShapeDtypeStruct((T, 512), BF16), jax.ShapeDtypeStruct((T, 768), BF16),
                   jax.ShapeDtypeStruct((T, 3072), BF16), jax.ShapeDtypeStruct((T, 512), F32),
                   jax.ShapeDtypeStruct((T, LANES), F32)),
        compiler_params=_cparams("parallel"),
        name="inproj",
    )(x, g, w, bg, dtbx, dtb8, alog8)


def _attn_kernel(q_ref, k_ref, v_ref, bias_ref, lam_ref, g_ref, o_ref, *, out_scale):
    S = q_ref.shape[0]
    lam = lam_ref[...]
    g = g_ref[...]
    for i in range(S // TA):
        q = q_ref[i * TA:(i + 1) * TA, :]
        lane = lax.broadcasted_iota(jnp.int32, q.shape, 1)
        qs = (jnp.where(lane < A_QK_DIM, q, jnp.zeros_like(q)),
              jnp.where(lane >= A_QK_DIM, q, jnp.zeros_like(q)))
        pieces = []
        if i >= 2:
            pieces.append((0, (i - 1) * TA, None))
        if i >= 1:
            pieces.append(((i - 1) * TA, i * TA, 1))
        pieces.append((i * TA, (i + 1) * TA, 0))
        outs = []
        for qm in qs:
            ss = []
            for lo, hi, bi in pieces:
                s = _dot_nt(qm, k_ref[lo:hi, :])
                if bi is not None:
                    s = s + bias_ref[bi]
                ss.append(s)
            m = jnp.max(ss[0], axis=-1, keepdims=True)
            for s in ss[1:]:
                m = jnp.maximum(m, jnp.max(s, axis=-1, keepdims=True))
            l = jnp.zeros((TA, 1), F32)
            acc = jnp.zeros((TA, A_V_DIM), F32)
            for s, (lo, hi, _) in zip(ss, pieces):
                p = jnp.exp(s - m)
                l = l + jnp.sum(p, axis=-1, keepdims=True)
                acc = acc + _dot(p.astype(BF16), v_ref[lo:hi, :])
            outs.append(acc / l)
        o = outs[0] - lam * outs[1]
        o_ref[i * TA:(i + 1) * TA, :] = (_rms(o, g) * out_scale).astype(o_ref.dtype)


def _attention(qkv, bias, lam_row, subln_g, out_scale, B, S):
    qkv3 = qkv.reshape(B, S, 3 * A_WIDTH)
    seq = lambda off: pl.BlockSpec((None, S, LANES), lambda b, h: (b, 0, off + h))
    return pl.pallas_call(
        functools.partial(_attn_kernel, out_scale=out_scale),
        grid=(B, A_HEADS),
        in_specs=[seq(0), seq(A_HEADS), seq(2 * A_HEADS),
                  pl.BlockSpec((None, 2, TA, TA), lambda b, h: (h, 0, 0, 0)),
                  pl.BlockSpec((1, LANES), lambda b, h: (0, 0)),
                  pl.BlockSpec((1, LANES), lambda b, h: (0, 0))],
        out_specs=seq(0),
        out_shape=jax.ShapeDtypeStruct((B, S, A_WIDTH), BF16),
        compiler_params=_cparams("parallel", "parallel"),
        name="diff_attention",
    )(qkv3, qkv3, qkv3, bias, lam_row, subln_g)


def _pool_kernel(u_ref, w_ref, scale_ref, o_ref):
    S = u_ref.shape[0]
    row = lax.broadcasted_iota(jnp.int32, (S, POOL_GROUP_DIM), 0)
    t = (row + 1).astype(F32)
    for gi, win in enumerate(POOL_WINDOWS):
        sl = slice(gi * POOL_GROUP_DIM, (gi + 1) * POOL_GROUP_DIM)
        u = u_ref[:, sl].astype(F32)
        acc = u
        s = 1
        while s < win:
            acc = acc + _shift_rows(acc, s, row)
            s *= 2
        d = acc / jnp.minimum(t, float(win)) - u
        y = _dot(d.astype(BF16), w_ref[gi])
        o_ref[:, sl] = (y * scale_ref[:, sl]).astype(o_ref.dtype)


def _pool(u, w_group, scale, B, S):
    u3 = u.reshape(B, S, POOL_WIDTH)
    return pl.pallas_call(
        _pool_kernel,
        grid=(B,),
        in_specs=[pl.BlockSpec((None, S, POOL_WIDTH), lambda b: (b, 0, 0)),
                  pl.BlockSpec((len(POOL_WINDOWS), POOL_GROUP_DIM, POOL_GROUP_DIM),
                               lambda b: (0, 0, 0)),
                  pl.BlockSpec((1, POOL_WIDTH), lambda b: (0, 0))],
        out_specs=pl.BlockSpec((None, S, POOL_WIDTH), lambda b: (b, 0, 0)),
        out_shape=jax.ShapeDtypeStruct((B, S, POOL_WIDTH), BF16),
        compiler_params=_cparams("parallel"),
        name="pool_mixer",
    )(u3, w_group, scale)


def _ssd_kernel(xbc_ref, z_ref, dtx_ref, r_ref, convw_ref, convb_ref, alogx_ref, dx_ref, g_ref,
                o_ref, conv_ref, csx_ref, state_ref):
    S = xbc_ref.shape[0]
    nc = S // CHUNK
    GN = SSD_GROUPS * SSD_STATE
    HP = SSD_WIDTH

    xin = xbc_ref[...].astype(F32)
    row = lax.broadcasted_iota(jnp.int32, xin.shape, 0)
    acc = xin * convw_ref[SSD_CONV - 1:SSD_CONV, :] + convb_ref[...]
    for s in range(1, SSD_CONV):
        acc = acc + _shift_rows(xin, s, row) * convw_ref[SSD_CONV - 1 - s:SSD_CONV - s, :]
    conv_ref[...] = _silu(acc)

    row_c = lax.broadcasted_iota(jnp.int32, (S, HP), 0) % CHUNK
    csx_ref[...] = _chunk_cumsum(dtx_ref[...] * (-jnp.exp(alogx_ref[...])), row_c)

    state_ref[...] = jnp.zeros(state_ref.shape, F32)

    li = lax.broadcasted_iota(jnp.int32, (CHUNK, HP), 0)
    ci = lax.broadcasted_iota(jnp.int32, (CHUNK, HP), 1)
    tril = li >= (ci % CHUNK)
    br = lax.broadcasted_iota(jnp.int32, (HP, GN), 0)
    bc = lax.broadcasted_iota(jnp.int32, (HP, GN), 1)
    bmask = (br // (HP // SSD_GROUPS)) == (bc // SSD_STATE)
    sr = lax.broadcasted_iota(jnp.int32, (GN, HP), 0)
    sc = lax.broadcasted_iota(jnp.int32, (GN, HP), 1)
    smask = (sr // SSD_STATE) == (sc // (HP // SSD_GROUPS))
    GW = HP // SSD_GROUPS
    dr = lax.broadcasted_iota(jnp.int32, (GW, GW), 0)
    dc = lax.broadcasted_iota(jnp.int32, (GW, GW), 1)
    dmask = (dr // CHUNK) == (dc // SSD_HEAD_DIM)

    def chunk_body(c, carry):
        r0 = pl.multiple_of(c * CHUNK, CHUNK)
        rows = pl.ds(r0, CHUNK)
        xs = conv_ref[rows, 0:HP]
        Bc = conv_ref[rows, HP:HP + GN].astype(BF16)
        Cc = conv_ref[rows, HP + GN:HP + 2 * GN].astype(BF16)
        dtx = dtx_ref[rows, :]
        cs = csx_ref[rows, :]
        cs_row = r_ref[pl.ds(c, 1), :]
        cs_end = csx_ref[pl.ds(r0 + CHUNK - 1, 1), :]

        xdt = xs * dtx
        Lm = jnp.exp(jnp.where(tril, cs - cs_row, NEG_BIG))
        b_rep = jnp.where(bmask, jnp.concatenate([Bc] * SSD_HEADS, axis=0), jnp.zeros((), BF16))
        scores = _dot_nt(Cc, b_rep)
        G = (scores * Lm).astype(BF16)

        xdt_b = xdt.astype(BF16)
        y_parts = []
        for g in range(SSD_GROUPS):
            xg = xdt_b[:, g * GW:(g + 1) * GW]
            xbd = jnp.where(dmask, jnp.concatenate([xg] * (GW // CHUNK), axis=0),
                            jnp.zeros((), BF16))
            y_parts.append(_dot(G[:, g * GW:(g + 1) * GW], xbd))
        y = jnp.concatenate(y_parts, axis=1)

        st = state_ref[...]
        y = y + _dot(Cc, st.astype(BF16)) * jnp.exp(cs)
        xdec = (xdt * jnp.exp(cs_end - cs)).astype(BF16)
        upd = jnp.where(smask, _dot_tn(Bc, xdec), 0.0)
        state_ref[...] = st * jnp.exp(cs_end) + upd

        y = y + dx_ref[...] * xs
        y = y * _silu(z_ref[rows, :].astype(F32))
        o_ref[rows, :] = _rms(y, g_ref[...]).astype(o_ref.dtype)
        return carry

    lax.fori_loop(0, nc, chunk_body, 0, unroll=2)


def _ssd(xbc, z, dtx, r, conv_w, conv_b, alogx, dx, norm_g, B, S):
    nc = S // CHUNK
    b3 = lambda n: pl.BlockSpec((None, S, n), lambda b: (b, 0, 0))
    const = lambda shape: pl.BlockSpec(shape, lambda b: (0,) * len(shape))
    return pl.pallas_call(
        _ssd_kernel,
        grid=(B,),
        in_specs=[b3(SSD_CONV_CH), b3(SSD_WIDTH), b3(SSD_WIDTH),
                  pl.BlockSpec((None, nc, SSD_WIDTH), lambda b: (b, 0, 0)),
                  const((SSD_CONV, SSD_CONV_CH)), const((1, SSD_CONV_CH)),
                  const((1, SSD_WIDTH)), const((1, SSD_WIDTH)), const((1, SSD_WIDTH))],
        out_specs=b3(SSD_WIDTH),
        out_shape=jax.ShapeDtypeStruct((B, S, SSD_WIDTH), BF16),
        scratch_shapes=[pltpu.VMEM((S, SSD_CONV_CH), F32), pltpu.VMEM((S, SSD_WIDTH), F32),
                        pltpu.VMEM((SSD_GROUPS * SSD_STATE, SSD_WIDTH), F32)],
        compiler_params=_cparams("parallel"),
        name="ssd_mixer",
    )(xbc.reshape(B, S, SSD_CONV_CH), z.reshape(B, S, SSD_WIDTH), dtx.reshape(B, S, SSD_WIDTH),
      r, conv_w, conv_b, alogx, dx, norm_g)


def _memkv_kernel(mem_ref, g_ref, w_ref, kv_ref):
    kv_ref[...] = _dot(_rms(mem_ref[...], g_ref[...]).astype(BF16), w_ref[...]).astype(kv_ref.dtype)


def _memkv(mem, g_mem, w_xkv):
    B, M, D = mem.shape
    return pl.pallas_call(
        _memkv_kernel,
        grid=(B,),
        in_specs=[pl.BlockSpec((None, M, D), lambda b: (b, 0, 0)),
                  pl.BlockSpec((1, D), lambda b: (0, 0)),
                  pl.BlockSpec((D, 2 * X_WIDTH), lambda b: (0, 0))],
        out_specs=pl.BlockSpec((None, M, 2 * X_WIDTH), lambda b: (b, 0, 0)),
        out_shape=jax.ShapeDtypeStruct((B, M, 2 * X_WIDTH), BF16),
        compiler_params=_cparams("parallel"),
        name="mem_kv",
    )(mem, g_mem, w_xkv)


def _route(lg, le):
    lane = lax.broadcasted_iota(jnp.int32, lg.shape, 1)
    big = jnp.int32(LANES)
    lgm = jnp.where(lane < N_EGROUPS, lg, NEG_BIG)
    gmax = jnp.max(lgm, axis=-1, keepdims=True)
    g_p = 1.0 / jnp.sum(jnp.exp(lgm - gmax), axis=-1, keepdims=True)
    g_idx = jnp.min(jnp.where(lgm == gmax, lane, big), axis=-1, keepdims=True)
    in_grp = (lane // EXPERTS_PER_GROUP == g_idx) & (lane < N_EXPERTS)
    e1 = jnp.where(in_grp, le, NEG_BIG)
    v1 = jnp.max(e1, axis=-1, keepdims=True)
    i1 = jnp.min(jnp.where(e1 == v1, lane, big), axis=-1, keepdims=True)
    e2 = jnp.where(lane == i1, NEG_BIG, e1)
    v2 = jnp.max(e2, axis=-1, keepdims=True)
    i2 = jnp.min(jnp.where(e2 == v2, lane, big), axis=-1, keepdims=True)
    t = jnp.exp(v2 - v1)
    w1 = g_p / (1.0 + t)
    w2 = g_p * t / (1.0 + t)
    return lane, i1, i2, w1, w2


def _merge_kernel(x_ref, oa_ref, ob_ref, oc_ref, gate_ref, wb_ref, wmix_ref, gx_ref, wq_ref,
                  kv_ref, wo_ref, gm_ref, wr_ref, br_ref, x2_ref, hp_ref, route_ref, cnt_ref):
    D = x_ref.shape[1]
    merged = gate_ref[:, 0:D].astype(F32) * _dot(oa_ref[...], wb_ref[0])
    merged = merged + gate_ref[:, D:2 * D].astype(F32) * _dot(ob_ref[...], wb_ref[1])
    merged = merged + gate_ref[:, 2 * D:3 * D].astype(F32) * _dot(oc_ref[...], wb_ref[2])
    x1 = x_ref[...] + _dot(merged.astype(BF16), wmix_ref[...])

    hq = _rms(x1, gx_ref[...]).astype(BF16)
    q = (_dot(hq, wq_ref[...]) * (X_HEAD_DIM ** -0.5)).astype(BF16)
    k = kv_ref[:, 0:X_WIDTH]
    v = kv_ref[:, X_WIDTH:2 * X_WIDTH]
    qlane = lax.broadcasted_iota(jnp.int32, q.shape, 1) // X_HEAD_DIM
    vlane = lax.broadcasted_iota(jnp.int32, v.shape, 1) // X_HEAD_DIM
    o = jnp.zeros(q.shape, F32)
    for hd in range(X_HEADS):
        s = _dot_nt(jnp.where(qlane == hd, q, jnp.zeros_like(q)), k)
        p = jnp.exp(s - jnp.max(s, axis=-1, keepdims=True))
        p = p / jnp.sum(p, axis=-1, keepdims=True)
        o = o + _dot(p.astype(BF16), jnp.where(vlane == hd, v, jnp.zeros_like(v)))
    x2 = x1 + _dot(o.astype(BF16), wo_ref[...])
    x2_ref[...] = x2

    hmf = _rms(x2, gm_ref[...])
    _store_rows(hp_ref, hmf)
    hm = hmf.astype(BF16)
    logits = _dot(hm, wr_ref[...]) + br_ref[...]
    lane, i1, i2, w1, w2 = _route(logits[:, 0:LANES], logits[:, LANES:2 * LANES])

    @pl.when(pl.program_id(0) == 0)
    def _():
        cnt_ref[...] = jnp.zeros(cnt_ref.shape, F32)

    oh1 = (lane == i1).astype(F32)
    oh2 = (lane == i2).astype(F32)
    tm = oh1.shape[0]
    tr = lax.broadcasted_iota(jnp.int32, (tm, tm), 0)
    tc = lax.broadcasted_iota(jnp.int32, (tm, tm), 1)
    before = jnp.where(tr > tc, 1.0, 0.0).astype(BF16)
    prior = _dot(before, (oh1 + oh2).astype(BF16)) + cnt_ref[0:1, :]
    rank1 = jnp.sum(prior * oh1, axis=-1, keepdims=True)
    rank2 = jnp.sum(prior * oh2, axis=-1, keepdims=True)
    cnt_ref[...] = cnt_ref[...] + jnp.sum(oh1 + oh2, axis=0, keepdims=True)
    route = jnp.where(lane == 0, i1.astype(F32), 0.0)
    route = jnp.where(lane == 1, i2.astype(F32), route)
    route = jnp.where(lane == 2, w1, route)
    route = jnp.where(lane == 3, w2, route)
    route = jnp.where(lane == 4, rank1, route)
    route_ref[...] = jnp.where(lane == 5, rank2, route)


def _merge(x, oa, ob, oc, gates, wb, wmix, gx, wq, kv, wo, gm, wr, br, S):
    T, D = x.shape
    tm = TM_MERGE
    per_b = S // tm
    row = lambda n: pl.BlockSpec((tm, n), lambda i: (i, 0))
    const = lambda shape: pl.BlockSpec(shape, lambda i: (0,) * len(shape),
                                       pipeline_mode=pl.Buffered(1))
    return pl.pallas_call(
        _merge_kernel,
        grid=(T // tm,),
        in_specs=[row(D), row(512), row(512), row(512), row(3 * D),
                  const((N_BRANCH, 512, D)), const((D, D)), const((1, D)), const((D, X_WIDTH)),
                  pl.BlockSpec((None, N_MEM, 2 * X_WIDTH), lambda i: (i // per_b, 0, 0)),
                  const((X_WIDTH, D)), const((1, D)), const((D, 2 * LANES)), const((1, 2 * LANES))],
        out_specs=(row(D), pl.BlockSpec((tm * ROW_SUB, LANES), lambda i: (i, 0)), row(LANES),
                   pl.BlockSpec((8, LANES), lambda i: (0, 0))),
        out_shape=(jax.ShapeDtypeStruct((T, D), F32),
                   jax.ShapeDtypeStruct((T * ROW_SUB, LANES), F32),
                   jax.ShapeDtypeStruct((T, LANES), F32), jax.ShapeDtypeStruct((8, LANES), F32)),
        compiler_params=_cparams("arbitrary"),
        name="merge_xattn_router",
    )(x, oa, ob, oc, gates, wb, wmix, gx, wq, kv, wo, gm, wr, br)


def _plan(route, cnt, T):
    e = route[:, 0:2].astype(jnp.int32)
    rank = route[:, 4:6].astype(jnp.int32)
    counts = cnt[0, :N_EXPERTS].astype(jnp.int32)
    padded = ((counts + TME - 1) // TME) * TME
    ends = jnp.cumsum(padded)
    off = ends - padded
    onehot = e[..., None] == jnp.arange(N_EXPERTS, dtype=jnp.int32)
    slots = (jnp.sum(jnp.where(onehot, off, 0), axis=-1) + rank).reshape(2 * T)
    n_tiles = (2 * T) // TME + N_EXPERTS
    tile_start = jnp.arange(n_tiles, dtype=jnp.int32) * TME
    tile_expert = jnp.minimum(jnp.sum(tile_start[:, None] >= ends[None, :], axis=1),
                              N_EXPERTS - 1).astype(jnp.int32)
    n_used = ends[-1] // TME
    tile_src = jnp.where(jnp.arange(n_tiles) < n_used, jnp.arange(n_tiles), 0).astype(jnp.int32)
    pad_tile = jnp.where(padded > 0, ends - TME, -1).astype(jnp.int32)
    return slots, tile_expert, tile_src, n_used.reshape(1).astype(jnp.int32), pad_tile


def _row_copy(src, dst, i, j, sem):
    return pltpu.make_async_copy(_row_tile(src, i), _row_tile(dst, j), sem)


def _scatter_kernel(slots_ref, pad_ref, nu_ref, hp_ref, hs_ref, zbuf_ref, sem, zsem):
    tm = hp_ref.shape[0] // ROW_SUB

    def zero_tile(row):
        start = pl.multiple_of(row * ROW_SUB, TME * ROW_SUB)
        return pltpu.make_async_copy(zbuf_ref, hs_ref.at[pl.ds(start, TME * ROW_SUB)], zsem)

    @pl.when(pl.program_id(0) == 0)
    def _():
        zbuf_ref[...] = jnp.zeros(zbuf_ref.shape, zbuf_ref.dtype)
        for e in range(N_EXPERTS):
            @pl.when(pad_ref[e] >= 0)
            def _():
                zero_tile(pad_ref[e]).start()
        for e in range(N_EXPERTS):
            @pl.when(pad_ref[e] >= 0)
            def _():
                zero_tile(pad_ref[e]).wait()

        def tail(t, c):
            zero_tile(t * TME).start()
            zero_tile(t * TME).wait()
            return c

        lax.fori_loop(nu_ref[0], hs_ref.shape[0] // (TME * ROW_SUB), tail, 0)

    def start(r, c):
        _row_copy(hp_ref, hs_ref, r, slots_ref[2 * r], sem).start(priority=0)
        _row_copy(hp_ref, hs_ref, r, slots_ref[2 * r + 1], sem).start(priority=1)
        return c

    def wait(r, c):
        _row_copy(hp_ref, hs_ref, r, slots_ref[2 * r], sem).wait()
        _row_copy(hp_ref, hs_ref, r, slots_ref[2 * r + 1], sem).wait()
        return c

    lax.fori_loop(0, tm, start, 0, unroll=8)
    lax.fori_loop(0, tm, wait, 0, unroll=8)


def _scatter(slots, pad_tile, n_used, hp, n_rows):
    T = hp.shape[0] // ROW_SUB
    tm = TM_MOE
    return pl.pallas_call(
        _scatter_kernel,
        grid=(T // tm,),
        in_specs=[pl.BlockSpec((2 * tm,), lambda i: (i,), memory_space=pltpu.SMEM),
                  pl.BlockSpec(memory_space=pltpu.SMEM),
                  pl.BlockSpec(memory_space=pltpu.SMEM),
                  pl.BlockSpec((tm * ROW_SUB, LANES), lambda i: (i, 0))],
        out_specs=pl.BlockSpec(memory_space=pl.ANY),
        out_shape=jax.ShapeDtypeStruct((n_rows * ROW_SUB, LANES), hp.dtype),
        scratch_shapes=[pltpu.VMEM((TME * ROW_SUB, LANES), hp.dtype), pltpu.SemaphoreType.DMA(()),
                        pltpu.SemaphoreType.DMA(())],
        compiler_params=_cparams("arbitrary"),
        name="moe_scatter",
    )(slots, pad_tile, n_used, hp)


def _expert_kernel(te_ref, ts_ref, nu_ref, hs_ref, wg_ref, wu_ref, wd_ref, ys_ref,
                   wgb_ref, wub_ref, wdb_ref):
    i = pl.program_id(0)
    active = i < nu_ref[0]
    new_expert = (i == 0) | (te_ref[i] != te_ref[jnp.maximum(i - 1, 0)])

    @pl.when(active & new_expert)
    def _():
        wgb_ref[...] = wg_ref[...].astype(BF16)
        wub_ref[...] = wu_ref[...].astype(BF16)
        wdb_ref[...] = wd_ref[...].astype(BF16)

    @pl.when(active)
    def _():
        h = _load_rows(hs_ref).astype(BF16)
        a = _silu(_dot(h, wgb_ref[...])) * _dot(h, wub_ref[...])
        _store_rows(ys_ref, _dot(a.astype(BF16), wdb_ref[...]))

    @pl.when(i >= nu_ref[0])
    def _():
        ys_ref[...] = jnp.zeros(ys_ref.shape, ys_ref.dtype)


def _experts(tile_expert, tile_src, n_used, hs, wg, wu, wd, layer):
    P = hs.shape[0] // ROW_SUB
    D = ROW_SUB * LANES
    wspec = lambda a, b: pl.BlockSpec((None, None, a, b),
                                      lambda i, te, ts, nu: (layer, te[i], 0, 0))
    grid_spec = pltpu.PrefetchScalarGridSpec(
        num_scalar_prefetch=3,
        grid=(P // TME,),
        in_specs=[pl.BlockSpec((TME * ROW_SUB, LANES), lambda i, te, ts, nu: (ts[i], 0)),
                  wspec(D, EXPERT_FF), wspec(D, EXPERT_FF), wspec(EXPERT_FF, D)],
        out_specs=pl.BlockSpec((TME * ROW_SUB, LANES), lambda i, te, ts, nu: (i, 0)),
        scratch_shapes=[pltpu.VMEM((D, EXPERT_FF), BF16), pltpu.VMEM((D, EXPERT_FF), BF16),
                        pltpu.VMEM((EXPERT_FF, D), BF16)])
    return pl.pallas_call(
        _expert_kernel,
        grid_spec=grid_spec,
        out_shape=jax.ShapeDtypeStruct((P * ROW_SUB, LANES), F32),
        compiler_params=_cparams("arbitrary"),
        name="moe_experts",
    )(tile_expert, tile_src, n_used, hs, wg, wu, wd)


def _combine_kernel(slots_ref, x_ref, route_ref, ys_ref, gf_ref, o_ref, ybuf_ref, sem, *, final):
    tm = x_ref.shape[0]

    def copy(r, k):
        return pltpu.make_async_copy(_row_tile(ys_ref, slots_ref[2 * r + k]),
                                     _row_tile(ybuf_ref.at[k], r), sem)

    def start(r, c):
        copy(r, 0).start(priority=0)
        copy(r, 1).start(priority=1)
        return c

    def wait(r, c):
        copy(r, 0).wait()
        copy(r, 1).wait()
        return c

    lax.fori_loop(0, tm, start, 0, unroll=8)
    lax.fori_loop(0, tm, wait, 0, unroll=8)
    route = route_ref[...]
    y = (x_ref[...] + route[:, 2:3] * _load_rows(ybuf_ref.at[0])
         + route[:, 3:4] * _load_rows(ybuf_ref.at[1]))
    o_ref[...] = _rms(y, gf_ref[...]) if final else y


def _combine(slots, x2, route, ys, g_final, final):
    T, D = x2.shape
    tm = TM_MOE
    row = lambda n: pl.BlockSpec((tm, n), lambda i: (i, 0))
    return pl.pallas_call(
        functools.partial(_combine_kernel, final=final),
        grid=(T // tm,),
        in_specs=[pl.BlockSpec((2 * tm,), lambda i: (i,), memory_space=pltpu.SMEM),
                  row(D), row(LANES), pl.BlockSpec(memory_space=pl.ANY),
                  pl.BlockSpec((1, D), lambda i: (0, 0))],
        out_specs=row(D),
        out_shape=jax.ShapeDtypeStruct((T, D), F32),
        scratch_shapes=[pltpu.VMEM((2, tm * ROW_SUB, LANES), F32),
                        pltpu.SemaphoreType.DMA(())],
        compiler_params=_cparams("arbitrary"),
        name="moe_combine",
    )(slots, x2, route, ys, g_final)


def _prep_inproj_weight(w_in_l):
    col_dt = 2560 + SSD_CONV_CH
    w_dt = w_in_l[:, col_dt:col_dt + SSD_HEADS]
    return jnp.concatenate(
        [w_in_l[:, :col_dt], w_in_l[:, col_dt + SSD_HEADS:],
         jnp.repeat(w_dt, SSD_HEAD_DIM, axis=1),
         jnp.pad(w_dt, ((0, 0), (0, LANES - SSD_HEADS)))], axis=1).astype(BF16)


def _pad_lanes(v, n=LANES):
    return jnp.pad(v, (0, n - v.shape[0]))[None, :]


def kernel(x, mem, w_in, b_gate, lam_q1, lam_k1, lam_q2, lam_k2, attn_subln_g, rel_bias_table,
           w_pool_group, pool_scale, ssd_conv_w, ssd_conv_b, ssd_dt_bias, ssd_A_log, ssd_D,
           ssd_norm_g, w_branch, w_mix_out, g_mix, g_xattn, g_mem, w_xq, w_xkv, w_xo, g_moe,
           w_router_group, b_router_group, w_router_expert, b_router_expert,
           w_exp_gate, w_exp_up, w_exp_down, g_final):
    B, S, D = x.shape
    T = B * S
    depth = w_in.shape[0]
    nc = S // CHUNK
    lam_inits = tuple(0.8 - 0.6 * math.exp(-0.3 * l) for l in range(depth))
    bias, lam = _setup(rel_bias_table, lam_q1, lam_k1, lam_q2, lam_k2, lam_inits)

    xf = x.reshape(T, D)
    for l in range(depth):
        w = _prep_inproj_weight(w_in[l])
        qkv, pool_u, z, xbc, gates, dtx, cs8 = _inproj(
            xf, g_mix[l][None, :], w, b_gate[l][None, :],
            jnp.repeat(ssd_dt_bias[l], SSD_HEAD_DIM)[None, :], _pad_lanes(ssd_dt_bias[l]),
            _pad_lanes(ssd_A_log[l]))
        o_a = _attention(qkv, bias, lam[l][None, :], attn_subln_g[l][None, :],
                         1.0 - lam_inits[l], B, S)
        o_b = _pool(pool_u, w_pool_group[l].astype(BF16), pool_scale[l][None, :], B, S)
        r = cs8[:, :SSD_HEADS].reshape(B, nc, CHUNK, SSD_HEADS).transpose(0, 1, 3, 2)
        r = r.reshape(B, nc, SSD_WIDTH)
        o_c = _ssd(xbc, z, dtx, r, ssd_conv_w[l], ssd_conv_b[l][None, :],
                   jnp.repeat(ssd_A_log[l], SSD_HEAD_DIM)[None, :],
                   jnp.repeat(ssd_D[l], SSD_HEAD_DIM)[None, :], ssd_norm_g[l][None, :], B, S)
        kv = _memkv(mem, g_mem[l][None, :], w_xkv[l].astype(BF16))
        w_r = jnp.concatenate(
            [jnp.pad(w_router_group[l], ((0, 0), (0, LANES - N_EGROUPS))),
             jnp.pad(w_router_expert[l], ((0, 0), (0, LANES - N_EXPERTS)))], axis=1).astype(BF16)
        b_r = jnp.concatenate([_pad_lanes(b_router_group[l]), _pad_lanes(b_router_expert[l])],
                              axis=1)
        x2, hp, route, cnt = _merge(
            xf, o_a.reshape(T, A_WIDTH), o_b.reshape(T, POOL_WIDTH), o_c.reshape(T, SSD_WIDTH),
            gates, w_branch[l].astype(BF16), w_mix_out[l].astype(BF16), g_xattn[l][None, :],
            w_xq[l].astype(BF16), kv, w_xo[l].astype(BF16), g_moe[l][None, :], w_r, b_r, S)
        slots, tile_expert, tile_src, n_used, pad_tile = _plan(route, cnt, T)
        hs = _scatter(slots, pad_tile, n_used, hp, 2 * T + N_EXPERTS * TME)
        ys = _experts(tile_expert, tile_src, n_used, hs, w_exp_gate, w_exp_up, w_exp_down, l)
        xf = _combine(slots, x2, route, ys, g_final[None, :], l == depth - 1)
    return xf.reshape(B, S, D)
```

```python
import functools
import math

import jax
import jax.numpy as jnp
from jax import lax
from jax.experimental import pallas as pl
from jax.experimental.pallas import tpu as pltpu

F32 = jnp.float32
BF16 = jnp.bfloat16

EPS = 1e-6
CHUNK = 64
N_MEM = 256

A_HEADS = 4
A_QK_DIM = 64
A_V_DIM = 128
A_WIDTH = 512
REL_BUCKETS = 32
REL_MAX_DIST = 128

POOL_WINDOWS = (2, 4, 8, 16)
POOL_GROUP_DIM = 128
POOL_WIDTH = 512

SSD_HEADS = 8
SSD_HEAD_DIM = 64
SSD_WIDTH = 512
SSD_GROUPS = 2
SSD_STATE = 64
SSD_CONV = 4
SSD_CONV_CH = 768

N_BRANCH = 3
X_HEADS = 4
X_HEAD_DIM = 64
X_WIDTH = 256

N_EGROUPS = 4
EXPERTS_PER_GROUP = 4
N_EXPERTS = 16
EXPERT_FF = 512

LANES = 128
NEG_BIG = -1e30

C_Q, C_K, C_V, C_POOL, C_Z, C_XBC, C_GATE = 0, 512, 1024, 1536, 2048, 2560, 3328
C_DTX = 6400
C_DT8 = 6912
C_END = 7040

TM_PROJ = 512
TM_MERGE = 512
TM_MOE = 512
TME = 256
TA = 256
VMEM_LIMIT = 56 * 1024 * 1024


def _cparams(*sem):
    return pltpu.CompilerParams(dimension_semantics=sem, vmem_limit_bytes=VMEM_LIMIT)


def _rms(x, g):
    return x * lax.rsqrt(jnp.mean(x * x, axis=-1, keepdims=True) + EPS) * g


def _dot(a, b):
    return jnp.dot(a, b, preferred_element_type=F32)


def _dot_nt(a, b):
    return lax.dot_general(a, b, (((1,), (1,)), ((), ())), preferred_element_type=F32)


def _dot_tn(a, b):
    return lax.dot_general(a, b, (((0,), (0,)), ((), ())), preferred_element_type=F32)


def _sigmoid(x):
    return 1.0 / (1.0 + jnp.exp(-x))


def _silu(x):
    return x * _sigmoid(x)


def _softplus(x):
    return jnp.maximum(x, 0.0) + jnp.log(1.0 + jnp.exp(-jnp.abs(x)))


ROW_SUB = 8


def _store_rows(ref, x):
    m = x.shape[0]
    for s in range(ROW_SUB):
        ref[pl.ds(s, m, stride=ROW_SUB), :] = x[:, s * LANES:(s + 1) * LANES]


def _load_rows(ref):
    m = ref.shape[0] // ROW_SUB
    return jnp.concatenate([ref[pl.ds(s, m, stride=ROW_SUB), :] for s in range(ROW_SUB)], axis=1)


def _row_tile(ref, r):
    return ref.at[pl.ds(pl.multiple_of(r * ROW_SUB, ROW_SUB), ROW_SUB)]


def _shift_rows(v, s, row):
    return jnp.where(row >= s, pltpu.roll(v, s, axis=0), 0.0)


def _chunk_cumsum(v, row_in_chunk):
    s = 1
    while s < CHUNK:
        v = v + jnp.where(row_in_chunk >= s, pltpu.roll(v, s, axis=0), 0.0)
        s *= 2
    return v


def _setup_kernel(table_ref, bucket_ref, vis_ref, lq1_ref, lk1_ref, lq2_ref, lk2_ref,
                  bias_ref, lam_ref, *, lam_inits):
    for h in range(A_HEADS):
        tiles = []
        for d in range(3):
            bkt = bucket_ref[d]
            acc = jnp.zeros(bkt.shape, F32)
            for b in range(REL_BUCKETS):
                acc = acc + jnp.where(bkt == b, table_ref[b, h], 0.0)
            tiles.append(acc)
        bias_ref[h, 0] = jnp.where(vis_ref[...] > 0, tiles[0] - tiles[2], NEG_BIG)
        bias_ref[h, 1] = tiles[1] - tiles[2]
    s1 = jnp.sum(lq1_ref[...] * lk1_ref[...], axis=-1, keepdims=True)
    s2 = jnp.sum(lq2_ref[...] * lk2_ref[...], axis=-1, keepdims=True)
    row = lax.broadcasted_iota(jnp.int32, s1.shape, 0)
    init = jnp.zeros(s1.shape, F32)
    for l, v in enumerate(lam_inits):
        init = jnp.where(row == l, v, init)
    lam_ref[...] = jnp.broadcast_to(jnp.exp(s1) - jnp.exp(s2) + init, lam_ref.shape)


def _rel_bucket_index(rel):
    nb = REL_BUCKETS // 2
    max_exact = nb // 2
    side = jnp.where(rel > 0, nb, 0)
    n = jnp.abs(rel)
    n_f = jnp.maximum(n, 1).astype(F32)
    large = max_exact + (jnp.log(n_f / max_exact) / math.log(REL_MAX_DIST / max_exact)
                         * (nb - max_exact)).astype(jnp.int32)
    large = jnp.minimum(large, nb - 1)
    return side + jnp.where(n < max_exact, n, large)


def _setup(rel_bias_table, lam_q1, lam_k1, lam_q2, lam_k2, lam_inits):
    depth = lam_q1.shape[0]
    r = jnp.arange(TA)
    rel0 = r[None, :] - r[:, None]
    bucket = jnp.stack([_rel_bucket_index(rel0 - d * TA) for d in range(3)]).astype(jnp.int32)
    vis = ((r[None, :] // CHUNK) <= (r[:, None] // CHUNK)).astype(jnp.int32)
    vm = pl.BlockSpec(memory_space=pltpu.VMEM)
    return pl.pallas_call(
        functools.partial(_setup_kernel, lam_inits=lam_inits),
        out_shape=(jax.ShapeDtypeStruct((A_HEADS, 2, TA, TA), F32),
                   jax.ShapeDtypeStruct((depth, LANES), F32)),
        in_specs=[pl.BlockSpec(memory_space=pltpu.SMEM), vm, vm, vm, vm, vm, vm],
        out_specs=(vm, vm),
        name="setup_bias_lam",
    )(rel_bias_table, bucket, vis, lam_q1, lam_k1, lam_q2, lam_k2)


def _gather_rows(slots_ref, ys_ref, ybuf_ref, sem, tm, wait):
    def body(r, c):
        for k in range(2):
            cp = pltpu.make_async_copy(_row_tile(ys_ref, slots_ref[2 * r + k]),
                                       _row_tile(ybuf_ref.at[k], r), sem)
            if wait:
                cp.wait()
            else:
                cp.start(priority=k)
        return c

    lax.fori_loop(0, tm, body, 0, unroll=8)


def _combined_residual(slots_ref, next_slots_ref, x2_ref, route_ref, ys_ref, ybuf_ref, sem, tm):
    i = pl.program_id(0)
    cur = i % 2

    @pl.when(i == 0)
    def _():
        _gather_rows(slots_ref, ys_ref, ybuf_ref.at[0], sem.at[0], tm, wait=False)

    @pl.when(i + 1 < pl.num_programs(0))
    def _():
        _gather_rows(next_slots_ref, ys_ref, ybuf_ref.at[1 - cur], sem.at[1 - cur], tm, wait=False)

    _gather_rows(slots_ref, ys_ref, ybuf_ref.at[cur], sem.at[cur], tm, wait=True)
    route = route_ref[...]
    return (x2_ref[...] + route[:, 2:3] * _load_rows(ybuf_ref.at[cur, 0])
            + route[:, 3:4] * _load_rows(ybuf_ref.at[cur, 1]))


def _inproj_kernel(*refs, combine):
    if combine:
        (slots_ref, next_slots_ref, x2_ref, route_ref, ys_ref, g_ref, w_ref, bg_ref, dtbx_ref,
         dtb8_ref, alog8_ref, x_ref, qkv_ref, pool_ref, z_ref, xbc_ref, gate_ref, dtx_ref,
         cs8_ref, ybuf_ref, sem) = refs
        x = _combined_residual(slots_ref, next_slots_ref, x2_ref, route_ref, ys_ref, ybuf_ref,
                               sem, x2_ref.shape[0])
        x_ref[...] = x
    else:
        (x_ref, g_ref, w_ref, bg_ref, dtbx_ref, dtb8_ref, alog8_ref,
         qkv_ref, pool_ref, z_ref, xbc_ref, gate_ref, dtx_ref, cs8_ref) = refs
        x = x_ref[...]
    h = _rms(x, g_ref[...]).astype(BF16)

    def mm(lo, hi):
        return _dot(h, w_ref[:, lo:hi])

    qkv_ref[:, 0:512] = (mm(C_Q, C_K) * (A_QK_DIM ** -0.5)).astype(BF16)
    qkv_ref[:, 512:1024] = mm(C_K, C_V).astype(BF16)
    qkv_ref[:, 1024:1536] = mm(C_V, C_POOL).astype(BF16)
    pool_ref[...] = mm(C_POOL, C_Z).astype(BF16)
    z_ref[...] = mm(C_Z, C_XBC).astype(BF16)
    xbc_ref[:, 0:512] = mm(C_XBC, C_XBC + 512).astype(BF16)
    xbc_ref[:, 512:768] = mm(C_XBC + 512, C_GATE).astype(BF16)
    for j in range(6):
        lo = C_GATE + 512 * j
        gate_ref[:, 512 * j:512 * (j + 1)] = _sigmoid(
            mm(lo, lo + 512) + bg_ref[:, 512 * j:512 * (j + 1)]).astype(BF16)
    dtx_ref[...] = _softplus(mm(C_DTX, C_DT8) + dtbx_ref[...])
    dt8 = _softplus(mm(C_DT8, C_END) + dtb8_ref[...])
    row = lax.broadcasted_iota(jnp.int32, dt8.shape, 0) % CHUNK
    cs8_ref[...] = _chunk_cumsum(dt8 * (-jnp.exp(alog8_ref[...])), row)


def _slot_specs(tm, n_steps):
    return [pl.BlockSpec((2 * tm,), lambda i: (i,), memory_space=pltpu.SMEM),
            pl.BlockSpec((2 * tm,), lambda i: (jnp.minimum(i + 1, n_steps - 1),),
                         memory_space=pltpu.SMEM)]


def _row_buffers(tm):
    return [pltpu.VMEM((2, 2, tm * ROW_SUB, LANES), F32), pltpu.SemaphoreType.DMA((2,))]


def _inproj(x, g, w, bg, dtbx, dtb8, alog8, layer, moe=None):
    T, D = x.shape
    tm = TM_PROJ
    n_steps = T // tm
    const = lambda shape: pl.BlockSpec(shape, lambda i: (0, 0), pipeline_mode=pl.Buffered(1))
    row = lambda n: pl.BlockSpec((tm, n), lambda i: (i, 0))
    in_specs = [row(D), const((1, D)),
                pl.BlockSpec((None, D, C_END), lambda i: (layer, 0, 0),
                             pipeline_mode=pl.Buffered(1)),
                const((1, 3072)), const((1, 512)), const((1, LANES)), const((1, LANES))]
    out_specs = [row(1536), row(512), row(512), row(768), row(3072), row(512), row(LANES)]
    out_shape = [jax.ShapeDtypeStruct((T, 1536), BF16), jax.ShapeDtypeStruct((T, 512), BF16),
                 jax.ShapeDtypeStruct((T, 512), BF16), jax.ShapeDtypeStruct((T, 768), BF16),
                 jax.ShapeDtypeStruct((T, 3072), BF16), jax.ShapeDtypeStruct((T, 512), F32),
                 jax.ShapeDtypeStruct((T, LANES), F32)]
    args = (x, g, w, bg, dtbx, dtb8, alog8)
    scratch = []
    if moe is not None:
        slots, route, ys = moe
        in_specs = (_slot_specs(tm, n_steps) + [row(D), row(LANES),
                                                pl.BlockSpec(memory_space=pl.ANY)] + in_specs[1:])
        out_specs = [row(D)] + out_specs
        out_shape = [jax.ShapeDtypeStruct((T, D), F32)] + out_shape
        args = (slots, slots, x, route, ys) + args[1:]
        scratch = _row_buffers(tm)
    return pl.pallas_call(
        functools.partial(_inproj_kernel, combine=moe is not None),
        grid=(n_steps,),
        in_specs=in_specs,
        out_specs=tuple(out_specs),
        out_shape=tuple(out_shape),
        scratch_shapes=scratch,
        compiler_params=_cparams("arbitrary" if moe is not None else "parallel"),
        name="inproj",
    )(*args)


def _attn_kernel(q_ref, k_ref, v_ref, bias_ref, lam_ref, g_ref, o_ref, vext_ref, *, out_scale):
    S = q_ref.shape[0]
    lam = lam_ref[...]
    g = g_ref[...]
    vext_ref[:, 0:A_V_DIM] = v_ref[...]
    vext_ref[:, A_V_DIM:2 * A_V_DIM] = jnp.ones((S, A_V_DIM), vext_ref.dtype)
    for i in range(S // TA):
        q = q_ref[i * TA:(i + 1) * TA, :]
        lane = lax.broadcasted_iota(jnp.int32, q.shape, 1)
        qq = jnp.concatenate([jnp.where(lane < A_QK_DIM, q, jnp.zeros_like(q)),
                              jnp.where(lane >= A_QK_DIM, q, jnp.zeros_like(q))], axis=0)
        pieces = []
        if i >= 2:
            pieces.append((0, (i - 1) * TA, None))
        if i >= 1:
            pieces.append(((i - 1) * TA, i * TA, 1))
        pieces.append((i * TA, (i + 1) * TA, 0))
        ss = []
        for lo, hi, bi in pieces:
            s = _dot_nt(qq, k_ref[lo:hi, :])
            if bi is not None:
                s = s + jnp.concatenate([bias_ref[bi], bias_ref[bi]], axis=0)
            ss.append(s)
        m = jnp.max(ss[0], axis=-1, keepdims=True)
        for s in ss[1:]:
            m = jnp.maximum(m, jnp.max(s, axis=-1, keepdims=True))
        acc = jnp.zeros((2 * TA, 2 * A_V_DIM), F32)
        for s, (lo, hi, _) in zip(ss, pieces):
            acc = acc + _dot(jnp.exp(s - m).astype(BF16), vext_ref[lo:hi, :])
        out = acc[:, 0:A_V_DIM] / acc[:, A_V_DIM:A_V_DIM + 1]
        o = out[0:TA] - lam * out[TA:2 * TA]
        o_ref[i * TA:(i + 1) * TA, :] = (_rms(o, g) * out_scale).astype(o_ref.dtype)


def _attention(qkv, bias, lam_row, subln_g, out_scale, B, S):
    qkv3 = qkv.reshape(B, S, 3 * A_WIDTH)
    seq = lambda off: pl.BlockSpec((None, S, LANES), lambda b, h: (b, 0, off + h))
    return pl.pallas_call(
        functools.partial(_attn_kernel, out_scale=out_scale),
        grid=(B, A_HEADS),
        in_specs=[seq(0), seq(A_HEADS), seq(2 * A_HEADS),
                  pl.BlockSpec((None, 2, TA, TA), lambda b, h: (h, 0, 0, 0)),
                  pl.BlockSpec((1, LANES), lambda b, h: (0, 0)),
                  pl.BlockSpec((1, LANES), lambda b, h: (0, 0))],
        out_specs=seq(0),
        out_shape=jax.ShapeDtypeStruct((B, S, A_WIDTH), BF16),
        scratch_shapes=[pltpu.VMEM((S, 2 * A_V_DIM), BF16)],
        compiler_params=_cparams("parallel", "parallel"),
        name="diff_attention",
    )(qkv3, qkv3, qkv3, bias, lam_row, subln_g)


def _pool_kernel(u_ref, w_ref, scale_ref, o_ref):
    S = u_ref.shape[0]
    row = lax.broadcasted_iota(jnp.int32, (S, POOL_GROUP_DIM), 0)
    t = (row + 1).astype(F32)
    for gi, win in enumerate(POOL_WINDOWS):
        sl = slice(gi * POOL_GROUP_DIM, (gi + 1) * POOL_GROUP_DIM)
        u = u_ref[:, sl].astype(F32)
        acc = u
        s = 1
        while s < win:
            acc = acc + _shift_rows(acc, s, row)
            s *= 2
        d = acc / jnp.minimum(t, float(win)) - u
        y = _dot(d.astype(BF16), w_ref[gi])
        o_ref[:, sl] = (y * scale_ref[:, sl]).astype(o_ref.dtype)


def _pool(u, w_group, scale, B, S):
    u3 = u.reshape(B, S, POOL_WIDTH)
    return pl.pallas_call(
        _pool_kernel,
        grid=(B,),
        in_specs=[pl.BlockSpec((None, S, POOL_WIDTH), lambda b: (b, 0, 0)),
                  pl.BlockSpec((len(POOL_WINDOWS), POOL_GROUP_DIM, POOL_GROUP_DIM),
                               lambda b: (0, 0, 0)),
                  pl.BlockSpec((1, POOL_WIDTH), lambda b: (0, 0))],
        out_specs=pl.BlockSpec((None, S, POOL_WIDTH), lambda b: (b, 0, 0)),
        out_shape=jax.ShapeDtypeStruct((B, S, POOL_WIDTH), BF16),
        compiler_params=_cparams("parallel"),
        name="pool_mixer",
    )(u3, w_group, scale)


def _ssd_kernel(xbc_ref, z_ref, dtx_ref, r_ref, convw_ref, convb_ref, alogx_ref, dx_ref, g_ref,
                o_ref, conv_ref, csx_ref, state_ref):
    S = xbc_ref.shape[0]
    nc = S // CHUNK
    GN = SSD_GROUPS * SSD_STATE
    HP = SSD_WIDTH

    xin = xbc_ref[...].astype(F32)
    row = lax.broadcasted_iota(jnp.int32, xin.shape, 0)
    acc = xin * convw_ref[SSD_CONV - 1:SSD_CONV, :] + convb_ref[...]
    for s in range(1, SSD_CONV):
        acc = acc + _shift_rows(xin, s, row) * convw_ref[SSD_CONV - 1 - s:SSD_CONV - s, :]
    conv_ref[...] = _silu(acc)

    row_c = lax.broadcasted_iota(jnp.int32, (S, HP), 0) % CHUNK
    csx_ref[...] = _chunk_cumsum(dtx_ref[...] * (-jnp.exp(alogx_ref[...])), row_c)

    state_ref[...] = jnp.zeros(state_ref.shape, F32)

    li = lax.broadcasted_iota(jnp.int32, (CHUNK, HP), 0)
    ci = lax.broadcasted_iota(jnp.int32, (CHUNK, HP), 1)
    tril = li >= (ci % CHUNK)
    br = lax.broadcasted_iota(jnp.int32, (HP, GN), 0)
    bc = lax.broadcasted_iota(jnp.int32, (HP, GN), 1)
    bmask = (br // (HP // SSD_GROUPS)) == (bc // SSD_STATE)
    sr = lax.broadcasted_iota(jnp.int32, (GN, HP), 0)
    sc = lax.broadcasted_iota(jnp.int32, (GN, HP), 1)
    smask = (sr // SSD_STATE) == (sc // (HP // SSD_GROUPS))
    GW = HP // SSD_GROUPS
    dr = lax.broadcasted_iota(jnp.int32, (GW, GW), 0)
    dc = lax.broadcasted_iota(jnp.int32, (GW, GW), 1)
    dmask = (dr // CHUNK) == (dc // SSD_HEAD_DIM)

    def chunk_body(c, carry):
        r0 = pl.multiple_of(c * CHUNK, CHUNK)
        rows = pl.ds(r0, CHUNK)
        xs = conv_ref[rows, 0:HP]
        Bc = conv_ref[rows, HP:HP + GN].astype(BF16)
        Cc = conv_ref[rows, HP + GN:HP + 2 * GN].astype(BF16)
        dtx = dtx_ref[rows, :]
        cs = csx_ref[rows, :]
        cs_row = r_ref[pl.ds(c, 1), :]
        cs_end = csx_ref[pl.ds(r0 + CHUNK - 1, 1), :]

        xdt = xs * dtx
        Lm = jnp.exp(jnp.where(tril, cs - cs_row, NEG_BIG))
        b_rep = jnp.where(bmask, jnp.concatenate([Bc] * SSD_HEADS, axis=0), jnp.zeros((), BF16))
        scores = _dot_nt(Cc, b_rep)
        G = (scores * Lm).astype(BF16)

        xdt_b = xdt.astype(BF16)
        y_parts = []
        for g in range(SSD_GROUPS):
            xg = xdt_b[:, g * GW:(g + 1) * GW]
            xbd = jnp.where(dmask, jnp.concatenate([xg] * (GW // CHUNK), axis=0),
                            jnp.zeros((), BF16))
            y_parts.append(_dot(G[:, g * GW:(g + 1) * GW], xbd))
        y = jnp.concatenate(y_parts, axis=1)

        st = state_ref[...]
        y = y + _dot(Cc, st.astype(BF16)) * jnp.exp(cs)
        xdec = (xdt * jnp.exp(cs_end - cs)).astype(BF16)
        upd = jnp.where(smask, _dot_tn(Bc, xdec), 0.0)
        state_ref[...] = st * jnp.exp(cs_end) + upd

        y = y + dx_ref[...] * xs
        y = y * _silu(z_ref[rows, :].astype(F32))
        o_ref[rows, :] = _rms(y, g_ref[...]).astype(o_ref.dtype)
        return carry

    lax.fori_loop(0, nc, chunk_body, 0, unroll=2)


def _ssd(xbc, z, dtx, r, conv_w, conv_b, alogx, dx, norm_g, B, S):
    nc = S // CHUNK
    b3 = lambda n: pl.BlockSpec((None, S, n), lambda b: (b, 0, 0))
    const = lambda shape: pl.BlockSpec(shape, lambda b: (0,) * len(shape))
    return pl.pallas_call(
        _ssd_kernel,
        grid=(B,),
        in_specs=[b3(SSD_CONV_CH), b3(SSD_WIDTH), b3(SSD_WIDTH),
                  pl.BlockSpec((None, nc, SSD_WIDTH), lambda b: (b, 0, 0)),
                  const((SSD_CONV, SSD_CONV_CH)), const((1, SSD_CONV_CH)),
                  const((1, SSD_WIDTH)), const((1, SSD_WIDTH)), const((1, SSD_WIDTH))],
        out_specs=b3(SSD_WIDTH),
        out_shape=jax.ShapeDtypeStruct((B, S, SSD_WIDTH), BF16),
        scratch_shapes=[pltpu.VMEM((S, SSD_CONV_CH), F32), pltpu.VMEM((S, SSD_WIDTH), F32),
                        pltpu.VMEM((SSD_GROUPS * SSD_STATE, SSD_WIDTH), F32)],
        compiler_params=_cparams("parallel"),
        name="ssd_mixer",
    )(xbc.reshape(B, S, SSD_CONV_CH), z.reshape(B, S, SSD_WIDTH), dtx.reshape(B, S, SSD_WIDTH),
      r, conv_w, conv_b, alogx, dx, norm_g)


def _memkv_kernel(mem_ref, g_ref, w_ref, kv_ref):
    kv_ref[...] = _dot(_rms(mem_ref[...], g_ref[...]).astype(BF16), w_ref[...]).astype(kv_ref.dtype)


def _memkv(mem, g_mem, w_xkv, layer):
    B, M, D = mem.shape
    return pl.pallas_call(
        _memkv_kernel,
        grid=(B,),
        in_specs=[pl.BlockSpec((None, M, D), lambda b: (b, 0, 0)),
                  pl.BlockSpec((1, D), lambda b: (0, 0)),
                  pl.BlockSpec((None, D, 2 * X_WIDTH), lambda b: (layer, 0, 0))],
        out_specs=pl.BlockSpec((None, M, 2 * X_WIDTH), lambda b: (b, 0, 0)),
        out_shape=jax.ShapeDtypeStruct((B, M, 2 * X_WIDTH), BF16),
        compiler_params=_cparams("parallel"),
        name="mem_kv",
    )(mem, g_mem, w_xkv)


def _route(lg, le):
    lane = lax.broadcasted_iota(jnp.int32, lg.shape, 1)
    big = jnp.int32(LANES)
    lgm = jnp.where(lane < N_EGROUPS, lg, NEG_BIG)
    gmax = jnp.max(lgm, axis=-1, keepdims=True)
    g_p = 1.0 / jnp.sum(jnp.exp(lgm - gmax), axis=-1, keepdims=True)
    g_idx = jnp.min(jnp.where(lgm == gmax, lane, big), axis=-1, keepdims=True)
    in_grp = (lane // EXPERTS_PER_GROUP == g_idx) & (lane < N_EXPERTS)
    e1 = jnp.where(in_grp, le, NEG_BIG)
    v1 = jnp.max(e1, axis=-1, keepdims=True)
    i1 = jnp.min(jnp.where(e1 == v1, lane, big), axis=-1, keepdims=True)
    e2 = jnp.where(lane == i1, NEG_BIG, e1)
    v2 = jnp.max(e2, axis=-1, keepdims=True)
    i2 = jnp.min(jnp.where(e2 == v2, lane, big), axis=-1, keepdims=True)
    t = jnp.exp(v2 - v1)
    w1 = g_p / (1.0 + t)
    w2 = g_p * t / (1.0 + t)
    return lane, i1, i2, w1, w2


def _merge_kernel(x_ref, oa_ref, ob_ref, oc_ref, gate_ref, wb_ref, wmix_ref, gx_ref, wq_ref,
                  kv_ref, wo_ref, gm_ref, wr_ref, br_ref, x2_ref, hp_ref, route_ref, cnt_ref):
    D = x_ref.shape[1]
    merged = gate_ref[:, 0:D].astype(F32) * _dot(oa_ref[...], wb_ref[0])
    merged = merged + gate_ref[:, D:2 * D].astype(F32) * _dot(ob_ref[...], wb_ref[1])
    merged = merged + gate_ref[:, 2 * D:3 * D].astype(F32) * _dot(oc_ref[...], wb_ref[2])
    x1 = x_ref[...] + _dot(merged.astype(BF16), wmix_ref[...])

    hq = _rms(x1, gx_ref[...]).astype(BF16)
    q = (_dot(hq, wq_ref[...]) * (X_HEAD_DIM ** -0.5)).astype(BF16)
    k = kv_ref[:, 0:X_WIDTH]
    v = kv_ref[:, X_WIDTH:2 * X_WIDTH]
    qlane = lax.broadcasted_iota(jnp.int32, q.shape, 1) // X_HEAD_DIM
    vlane = lax.broadcasted_iota(jnp.int32, v.shape, 1) // X_HEAD_DIM
    o = jnp.zeros(q.shape, F32)
    for hd in range(X_HEADS):
        s = _dot_nt(jnp.where(qlane == hd, q, jnp.zeros_like(q)), k)
        p = jnp.exp(s - jnp.max(s, axis=-1, keepdims=True))
        p = p / jnp.sum(p, axis=-1, keepdims=True)
        o = o + _dot(p.astype(BF16), jnp.where(vlane == hd, v, jnp.zeros_like(v)))
    x2 = x1 + _dot(o.astype(BF16), wo_ref[...])
    x2_ref[...] = x2

    hmf = _rms(x2, gm_ref[...])
    _store_rows(hp_ref, hmf)
    hm = hmf.astype(BF16)
    logits = _dot(hm, wr_ref[...]) + br_ref[...]
    lane, i1, i2, w1, w2 = _route(logits[:, 0:LANES], logits[:, LANES:2 * LANES])

    @pl.when(pl.program_id(0) == 0)
    def _():
        cnt_ref[...] = jnp.zeros(cnt_ref.shape, F32)

    oh1 = (lane == i1).astype(F32)
    oh2 = (lane == i2).astype(F32)
    tm = oh1.shape[0]
    tr = lax.broadcasted_iota(jnp.int32, (tm, tm), 0)
    tc = lax.broadcasted_iota(jnp.int32, (tm, tm), 1)
    before = jnp.where(tr > tc, 1.0, 0.0).astype(BF16)
    prior = _dot(before, (oh1 + oh2).astype(BF16)) + cnt_ref[0:1, :]
    rank1 = jnp.sum(prior * oh1, axis=-1, keepdims=True)
    rank2 = jnp.sum(prior * oh2, axis=-1, keepdims=True)
    cnt_ref[...] = cnt_ref[...] + jnp.sum(oh1 + oh2, axis=0, keepdims=True)
    route = jnp.where(lane == 0, i1.astype(F32), 0.0)
    route = jnp.where(lane == 1, i2.astype(F32), route)
    route = jnp.where(lane == 2, w1, route)
    route = jnp.where(lane == 3, w2, route)
    route = jnp.where(lane == 4, rank1, route)
    route_ref[...] = jnp.where(lane == 5, rank2, route)


def _merge(x, oa, ob, oc, gates, wb, wmix, gx, wq, kv, wo, gm, wr, br, S, layer):
    T, D = x.shape
    tm = TM_MERGE
    per_b = S // tm
    row = lambda n: pl.BlockSpec((tm, n), lambda i: (i, 0))
    const = lambda shape: pl.BlockSpec(shape, lambda i: (0,) * len(shape),
                                       pipeline_mode=pl.Buffered(1))
    stack = lambda shape: pl.BlockSpec((None,) + shape, lambda i: (layer,) + (0,) * len(shape),
                                       pipeline_mode=pl.Buffered(1))
    return pl.pallas_call(
        _merge_kernel,
        grid=(T // tm,),
        in_specs=[row(D), row(512), row(512), row(512), row(3 * D),
                  stack((N_BRANCH, 512, D)), stack((D, D)), const((1, D)), stack((D, X_WIDTH)),
                  pl.BlockSpec((None, N_MEM, 2 * X_WIDTH), lambda i: (i // per_b, 0, 0)),
                  stack((X_WIDTH, D)), const((1, D)), stack((D, 2 * LANES)),
                  const((1, 2 * LANES))],
        out_specs=(row(D), pl.BlockSpec((tm * ROW_SUB, LANES), lambda i: (i, 0)), row(LANES),
                   pl.BlockSpec((8, LANES), lambda i: (0, 0))),
        out_shape=(jax.ShapeDtypeStruct((T, D), F32),
                   jax.ShapeDtypeStruct((T * ROW_SUB, LANES), F32),
                   jax.ShapeDtypeStruct((T, LANES), F32), jax.ShapeDtypeStruct((8, LANES), F32)),
        compiler_params=_cparams("arbitrary"),
        name="merge_xattn_router",
    )(x, oa, ob, oc, gates, wb, wmix, gx, wq, kv, wo, gm, wr, br)


def _plan(route, cnt, T):
    e = route[:, 0:2].astype(jnp.int32)
    rank = route[:, 4:6].astype(jnp.int32)
    counts = cnt[0, :N_EXPERTS].astype(jnp.int32)
    padded = ((counts + TME - 1) // TME) * TME
    ends = jnp.cumsum(padded)
    off = ends - padded
    onehot = e[..., None] == jnp.arange(N_EXPERTS, dtype=jnp.int32)
    slots = (jnp.sum(jnp.where(onehot, off, 0), axis=-1) + rank).reshape(2 * T)
    n_tiles = (2 * T) // TME + N_EXPERTS
    tile_start = jnp.arange(n_tiles, dtype=jnp.int32) * TME
    tile_expert = jnp.minimum(jnp.sum(tile_start[:, None] >= ends[None, :], axis=1),
                              N_EXPERTS - 1).astype(jnp.int32)
    n_used = ends[-1] // TME
    tile_src = jnp.where(jnp.arange(n_tiles) < n_used, jnp.arange(n_tiles), 0).astype(jnp.int32)
    pad_tile = jnp.where(padded > 0, ends - TME, -1).astype(jnp.int32)
    return slots, tile_expert, tile_src, n_used.reshape(1).astype(jnp.int32), pad_tile


def _row_copy(src, dst, i, j, sem):
    return pltpu.make_async_copy(_row_tile(src, i), _row_tile(dst, j), sem)


def _scatter_kernel(slots_ref, pad_ref, nu_ref, hp_ref, hs_ref, zbuf_ref, sem, zsem):
    tm = hp_ref.shape[0] // ROW_SUB

    def zero_tile(row):
        start = pl.multiple_of(row * ROW_SUB, TME * ROW_SUB)
        return pltpu.make_async_copy(zbuf_ref, hs_ref.at[pl.ds(start, TME * ROW_SUB)], zsem)

    @pl.when(pl.program_id(0) == 0)
    def _():
        zbuf_ref[...] = jnp.zeros(zbuf_ref.shape, zbuf_ref.dtype)
        for e in range(N_EXPERTS):
            @pl.when(pad_ref[e] >= 0)
            def _():
                zero_tile(pad_ref[e]).start()
        for e in range(N_EXPERTS):
            @pl.when(pad_ref[e] >= 0)
            def _():
                zero_tile(pad_ref[e]).wait()

        def tail(t, c):
            zero_tile(t * TME).start()
            zero_tile(t * TME).wait()
            return c

        lax.fori_loop(nu_ref[0], hs_ref.shape[0] // (TME * ROW_SUB), tail, 0)

    def start(r, c):
        _row_copy(hp_ref, hs_ref, r, slots_ref[2 * r], sem).start(priority=0)
        _row_copy(hp_ref, hs_ref, r, slots_ref[2 * r + 1], sem).start(priority=1)
        return c

    def wait(r, c):
        _row_copy(hp_ref, hs_ref, r, slots_ref[2 * r], sem).wait()
        _row_copy(hp_ref, hs_ref, r, slots_ref[2 * r + 1], sem).wait()
        return c

    lax.fori_loop(0, tm, start, 0, unroll=8)
    lax.fori_loop(0, tm, wait, 0, unroll=8)


def _scatter(slots, pad_tile, n_used, hp, n_rows):
    T = hp.shape[0] // ROW_SUB
    tm = TM_MOE
    return pl.pallas_call(
        _scatter_kernel,
        grid=(T // tm,),
        in_specs=[pl.BlockSpec((2 * tm,), lambda i: (i,), memory_space=pltpu.SMEM),
                  pl.BlockSpec(memory_space=pltpu.SMEM),
                  pl.BlockSpec(memory_space=pltpu.SMEM),
                  pl.BlockSpec((tm * ROW_SUB, LANES), lambda i: (i, 0))],
        out_specs=pl.BlockSpec(memory_space=pl.ANY),
        out_shape=jax.ShapeDtypeStruct((n_rows * ROW_SUB, LANES), hp.dtype),
        scratch_shapes=[pltpu.VMEM((TME * ROW_SUB, LANES), hp.dtype), pltpu.SemaphoreType.DMA(()),
                        pltpu.SemaphoreType.DMA(())],
        compiler_params=_cparams("arbitrary"),
        name="moe_scatter",
    )(slots, pad_tile, n_used, hp)


def _expert_kernel(te_ref, ts_ref, nu_ref, hs_ref, wg_ref, wu_ref, wd_ref, ys_ref,
                   wgb_ref, wub_ref, wdb_ref):
    i = pl.program_id(0)
    active = i < nu_ref[0]
    new_expert = (i == 0) | (te_ref[i] != te_ref[jnp.maximum(i - 1, 0)])

    @pl.when(active & new_expert)
    def _():
        wgb_ref[...] = wg_ref[...].astype(BF16)
        wub_ref[...] = wu_ref[...].astype(BF16)
        wdb_ref[...] = wd_ref[...].astype(BF16)

    @pl.when(active)
    def _():
        h = _load_rows(hs_ref).astype(BF16)
        a = _silu(_dot(h, wgb_ref[...])) * _dot(h, wub_ref[...])
        _store_rows(ys_ref, _dot(a.astype(BF16), wdb_ref[...]))

    @pl.when(i >= nu_ref[0])
    def _():
        ys_ref[...] = jnp.zeros(ys_ref.shape, ys_ref.dtype)


def _experts(tile_expert, tile_src, n_used, hs, wg, wu, wd, layer):
    P = hs.shape[0] // ROW_SUB
    D = ROW_SUB * LANES
    wspec = lambda a, b: pl.BlockSpec((None, None, a, b),
                                      lambda i, te, ts, nu: (layer, te[i], 0, 0))
    grid_spec = pltpu.PrefetchScalarGridSpec(
        num_scalar_prefetch=3,
        grid=(P // TME,),
        in_specs=[pl.BlockSpec((TME * ROW_SUB, LANES), lambda i, te, ts, nu: (ts[i], 0)),
                  wspec(D, EXPERT_FF), wspec(D, EXPERT_FF), wspec(EXPERT_FF, D)],
        out_specs=pl.BlockSpec((TME * ROW_SUB, LANES), lambda i, te, ts, nu: (i, 0)),
        scratch_shapes=[pltpu.VMEM((D, EXPERT_FF), BF16), pltpu.VMEM((D, EXPERT_FF), BF16),
                        pltpu.VMEM((EXPERT_FF, D), BF16)])
    return pl.pallas_call(
        _expert_kernel,
        grid_spec=grid_spec,
        out_shape=jax.ShapeDtypeStruct((P * ROW_SUB, LANES), F32),
        compiler_params=_cparams("arbitrary"),
        name="moe_experts",
    )(tile_expert, tile_src, n_used, hs, wg, wu, wd)


def _final_kernel(slots_ref, next_slots_ref, x2_ref, route_ref, ys_ref, gf_ref, o_ref,
                  ybuf_ref, sem):
    y = _combined_residual(slots_ref, next_slots_ref, x2_ref, route_ref, ys_ref, ybuf_ref, sem,
                           x2_ref.shape[0])
    o_ref[...] = _rms(y, gf_ref[...])


def _final(slots, x2, route, ys, g_final):
    T, D = x2.shape
    tm = TM_MOE
    n_steps = T // tm
    row = lambda n: pl.BlockSpec((tm, n), lambda i: (i, 0))
    return pl.pallas_call(
        _final_kernel,
        grid=(n_steps,),
        in_specs=_slot_specs(tm, n_steps) + [row(D), row(LANES), pl.BlockSpec(memory_space=pl.ANY),
                                             pl.BlockSpec((1, D), lambda i: (0, 0))],
        out_specs=row(D),
        out_shape=jax.ShapeDtypeStruct((T, D), F32),
        scratch_shapes=_row_buffers(tm),
        compiler_params=_cparams("arbitrary"),
        name="moe_combine_final",
    )(slots, slots, x2, route, ys, g_final)


def _prep_inproj_weight(w_in):
    col_dt = 2560 + SSD_CONV_CH
    w_dt = w_in[:, :, col_dt:col_dt + SSD_HEADS]
    return jnp.concatenate(
        [w_in[:, :, :col_dt], w_in[:, :, col_dt + SSD_HEADS:],
         jnp.repeat(w_dt, SSD_HEAD_DIM, axis=2),
         jnp.pad(w_dt, ((0, 0), (0, 0), (0, LANES - SSD_HEADS)))], axis=2).astype(BF16)


def _pad_lanes(v, n=LANES):
    return jnp.pad(v, (0, n - v.shape[0]))[None, :]


def kernel(x, mem, w_in, b_gate, lam_q1, lam_k1, lam_q2, lam_k2, attn_subln_g, rel_bias_table,
           w_pool_group, pool_scale, ssd_conv_w, ssd_conv_b, ssd_dt_bias, ssd_A_log, ssd_D,
           ssd_norm_g, w_branch, w_mix_out, g_mix, g_xattn, g_mem, w_xq, w_xkv, w_xo, g_moe,
           w_router_group, b_router_group, w_router_expert, b_router_expert,
           w_exp_gate, w_exp_up, w_exp_down, g_final):
    B, S, D = x.shape
    T = B * S
    depth = w_in.shape[0]
    nc = S // CHUNK
    lam_inits = tuple(0.8 - 0.6 * math.exp(-0.3 * l) for l in range(depth))
    bias, lam = _setup(rel_bias_table, lam_q1, lam_k1, lam_q2, lam_k2, lam_inits)

    xf = x.reshape(T, D)
    w_proj = _prep_inproj_weight(w_in)
    w_branch_b, w_mix_b = w_branch.astype(BF16), w_mix_out.astype(BF16)
    w_xq_b, w_xkv_b, w_xo_b = w_xq.astype(BF16), w_xkv.astype(BF16), w_xo.astype(BF16)
    w_r = jnp.concatenate(
        [jnp.pad(w_router_group, ((0, 0), (0, 0), (0, LANES - N_EGROUPS))),
         jnp.pad(w_router_expert, ((0, 0), (0, 0), (0, LANES - N_EXPERTS)))], axis=2).astype(BF16)
    moe = None
    for l in range(depth):
        outs = _inproj(
            xf, g_mix[l][None, :], w_proj, b_gate[l][None, :],
            jnp.repeat(ssd_dt_bias[l], SSD_HEAD_DIM)[None, :], _pad_lanes(ssd_dt_bias[l]),
            _pad_lanes(ssd_A_log[l]), l, moe)
        if moe is not None:
            xf, outs = outs[0], outs[1:]
        qkv, pool_u, z, xbc, gates, dtx, cs8 = outs
        o_a = _attention(qkv, bias, lam[l][None, :], attn_subln_g[l][None, :],
                         1.0 - lam_inits[l], B, S)
        o_b = _pool(pool_u, w_pool_group[l].astype(BF16), pool_scale[l][None, :], B, S)
        r = cs8[:, :SSD_HEADS].reshape(B, nc, CHUNK, SSD_HEADS).transpose(0, 1, 3, 2)
        r = r.reshape(B, nc, SSD_WIDTH)
        o_c = _ssd(xbc, z, dtx, r, ssd_conv_w[l], ssd_conv_b[l][None, :],
                   jnp.repeat(ssd_A_log[l], SSD_HEAD_DIM)[None, :],
                   jnp.repeat(ssd_D[l], SSD_HEAD_DIM)[None, :], ssd_norm_g[l][None, :], B, S)
        kv = _memkv(mem, g_mem[l][None, :], w_xkv_b, l)
        b_r = jnp.concatenate([_pad_lanes(b_router_group[l]), _pad_lanes(b_router_expert[l])],
                              axis=1)
        x2, hp, route, cnt = _merge(
            xf, o_a.reshape(T, A_WIDTH), o_b.reshape(T, POOL_WIDTH), o_c.reshape(T, SSD_WIDTH),
            gates, w_branch_b, w_mix_b, g_xattn[l][None, :], w_xq_b, kv, w_xo_b,
            g_moe[l][None, :], w_r, b_r, S, l)
        slots, tile_expert, tile_src, n_used, pad_tile = _plan(route, cnt, T)
        hs = _scatter(slots, pad_tile, n_used, hp, 2 * T + N_EXPERTS * TME)
        ys = _experts(tile_expert, tile_src, n_used, hs, w_exp_gate, w_exp_up, w_exp_down, l)
        xf, moe = x2, (slots, route, ys)
    return _final(moe[0], xf, moe[1], moe[2], g_final[None, :]).reshape(B, S, D)
```

```python
import functools
import math

import jax
import jax.numpy as jnp
from jax import lax
from jax.experimental import pallas as pl
from jax.experimental.pallas import tpu as pltpu

F32 = jnp.float32
BF16 = jnp.bfloat16

EPS = 1e-6
CHUNK = 64
N_MEM = 256

A_HEADS = 4
A_QK_DIM = 64
A_V_DIM = 128
A_WIDTH = 512
REL_BUCKETS = 32
REL_MAX_DIST = 128

POOL_WINDOWS = (2, 4, 8, 16)
POOL_GROUP_DIM = 128
POOL_WIDTH = 512

SSD_HEADS = 8
SSD_HEAD_DIM = 64
SSD_WIDTH = 512
SSD_GROUPS = 2
SSD_STATE = 64
SSD_CONV = 4
SSD_CONV_CH = 768

N_BRANCH = 3
X_HEADS = 4
X_HEAD_DIM = 64
X_WIDTH = 256

N_EGROUPS = 4
EXPERTS_PER_GROUP = 4
N_EXPERTS = 16
EXPERT_FF = 512

LANES = 128
NEG_BIG = -1e30

C_Q, C_K, C_V, C_POOL, C_Z, C_XBC, C_GATE = 0, 512, 1024, 1536, 2048, 2560, 3328
C_DTX = 6400
C_DT8 = 6912
C_END = 7040

TM_PROJ = 512
TM_MERGE = 512
TM_MOE = 512
TM_SCATTER = 1024
SLOT_BLOCK = 1024
TME = 256
N_PAIRS = 6
N_CLASSES = N_EGROUPS * N_PAIRS
H_SUB = 4
TA = 256
VMEM_LIMIT = 56 * 1024 * 1024


def _cparams(*sem):
    return pltpu.CompilerParams(dimension_semantics=sem, vmem_limit_bytes=VMEM_LIMIT)


def _rms(x, g):
    return x * lax.rsqrt(jnp.mean(x * x, axis=-1, keepdims=True) + EPS) * g


def _dot(a, b):
    return jnp.dot(a, b, preferred_element_type=F32)


def _dot_nt(a, b):
    return lax.dot_general(a, b, (((1,), (1,)), ((), ())), preferred_element_type=F32)


def _dot_tn(a, b):
    return lax.dot_general(a, b, (((0,), (0,)), ((), ())), preferred_element_type=F32)


def _sigmoid(x):
    return 1.0 / (1.0 + jnp.exp(-x))


def _silu(x):
    return x * _sigmoid(x)


def _softplus(x):
    return jnp.maximum(x, 0.0) + jnp.log(1.0 + jnp.exp(-jnp.abs(x)))


ROW_SUB = 8


def _store_rows(ref, x):
    m = x.shape[0]
    for s in range(ROW_SUB):
        ref[pl.ds(s, m, stride=ROW_SUB), :] = x[:, s * LANES:(s + 1) * LANES]


def _load_rows(ref):
    m = ref.shape[0] // ROW_SUB
    return jnp.concatenate([ref[pl.ds(s, m, stride=ROW_SUB), :] for s in range(ROW_SUB)], axis=1)


def _row_tile(ref, r):
    return ref.at[pl.ds(pl.multiple_of(r * ROW_SUB, ROW_SUB), ROW_SUB)]


def _shift_rows(v, s, row):
    return jnp.where(row >= s, pltpu.roll(v, s, axis=0), 0.0)


def _chunk_cumsum(v, row_in_chunk):
    s = 1
    while s < CHUNK:
        v = v + jnp.where(row_in_chunk >= s, pltpu.roll(v, s, axis=0), 0.0)
        s *= 2
    return v


def _setup_kernel(table_ref, bucket_ref, vis_ref, lq1_ref, lk1_ref, lq2_ref, lk2_ref,
                  bias_ref, lam_ref, *, lam_inits):
    for h in range(A_HEADS):
        tiles = []
        for d in range(3):
            bkt = bucket_ref[d]
            acc = jnp.zeros(bkt.shape, F32)
            for b in range(REL_BUCKETS):
                acc = acc + jnp.where(bkt == b, table_ref[b, h], 0.0)
            tiles.append(acc)
        bias_ref[h, 0] = jnp.where(vis_ref[...] > 0, tiles[0] - tiles[2], NEG_BIG)
        bias_ref[h, 1] = tiles[1] - tiles[2]
    s1 = jnp.sum(lq1_ref[...] * lk1_ref[...], axis=-1, keepdims=True)
    s2 = jnp.sum(lq2_ref[...] * lk2_ref[...], axis=-1, keepdims=True)
    row = lax.broadcasted_iota(jnp.int32, s1.shape, 0)
    init = jnp.zeros(s1.shape, F32)
    for l, v in enumerate(lam_inits):
        init = jnp.where(row == l, v, init)
    lam_ref[...] = jnp.broadcast_to(jnp.exp(s1) - jnp.exp(s2) + init, lam_ref.shape)


def _rel_bucket_index(rel):
    nb = REL_BUCKETS // 2
    max_exact = nb // 2
    side = jnp.where(rel > 0, nb, 0)
    n = jnp.abs(rel)
    n_f = jnp.maximum(n, 1).astype(F32)
    large = max_exact + (jnp.log(n_f / max_exact) / math.log(REL_MAX_DIST / max_exact)
                         * (nb - max_exact)).astype(jnp.int32)
    large = jnp.minimum(large, nb - 1)
    return side + jnp.where(n < max_exact, n, large)


def _setup(rel_bias_table, lam_q1, lam_k1, lam_q2, lam_k2, lam_inits):
    depth = lam_q1.shape[0]
    r = jnp.arange(TA)
    rel0 = r[None, :] - r[:, None]
    bucket = jnp.stack([_rel_bucket_index(rel0 - d * TA) for d in range(3)]).astype(jnp.int32)
    vis = ((r[None, :] // CHUNK) <= (r[:, None] // CHUNK)).astype(jnp.int32)
    vm = pl.BlockSpec(memory_space=pltpu.VMEM)
    return pl.pallas_call(
        functools.partial(_setup_kernel, lam_inits=lam_inits),
        out_shape=(jax.ShapeDtypeStruct((A_HEADS, 2, TA, TA), F32),
                   jax.ShapeDtypeStruct((depth, LANES), F32)),
        in_specs=[pl.BlockSpec(memory_space=pltpu.SMEM), vm, vm, vm, vm, vm, vm],
        out_specs=(vm, vm),
        name="setup_bias_lam",
    )(rel_bias_table, bucket, vis, lam_q1, lam_k1, lam_q2, lam_k2)


def _gather_rows(slots_ref, base, ys_ref, ybuf_ref, sem, tm, wait):
    def body(r2, c):
        for k in range(2):
            r = 2 * r2 + k
            cp = pltpu.make_async_copy(_row_tile(ys_ref, slots_ref[base + r]),
                                       _row_tile(ybuf_ref, r), sem)
            if wait:
                cp.wait()
            else:
                cp.start(priority=k)
        return c

    lax.fori_loop(0, tm // 2, body, 0, unroll=8)


def _combined_residual(slots_ref, next_slots_ref, x2_ref, ys_ref, ybuf_ref, sem, tm):
    i = pl.program_id(0)
    cur = i % 2
    per = SLOT_BLOCK // tm
    base = (i % per) * tm
    next_base = ((i + 1) % per) * tm

    @pl.when(i == 0)
    def _():
        _gather_rows(slots_ref, base, ys_ref, ybuf_ref.at[0], sem.at[0], tm, wait=False)

    @pl.when(i + 1 < pl.num_programs(0))
    def _():
        _gather_rows(next_slots_ref, next_base, ys_ref, ybuf_ref.at[1 - cur], sem.at[1 - cur], tm,
                     wait=False)

    _gather_rows(slots_ref, base, ys_ref, ybuf_ref.at[cur], sem.at[cur], tm, wait=True)
    return x2_ref[...] + _load_rows(ybuf_ref.at[cur])


def _inproj_kernel(*refs, combine):
    if combine:
        (slots_ref, next_slots_ref, x2_ref, ys_ref, g_ref, w_ref, bg_ref, dtbx_ref,
         dtb8_ref, alog8_ref, x_ref, qkv_ref, pool_ref, z_ref, xbc_ref, gate_ref, dtx_ref,
         cs8_ref, ybuf_ref, sem) = refs
        x = _combined_residual(slots_ref, next_slots_ref, x2_ref, ys_ref, ybuf_ref, sem,
                               x2_ref.shape[0])
        x_ref[...] = x
    else:
        (x_ref, g_ref, w_ref, bg_ref, dtbx_ref, dtb8_ref, alog8_ref,
         qkv_ref, pool_ref, z_ref, xbc_ref, gate_ref, dtx_ref, cs8_ref) = refs
        x = x_ref[...]
    h = _rms(x, g_ref[...]).astype(BF16)

    def mm(lo, hi):
        return _dot(h, w_ref[:, lo:hi])

    qkv_ref[:, 0:512] = (mm(C_Q, C_K) * (A_QK_DIM ** -0.5)).astype(BF16)
    qkv_ref[:, 512:1024] = mm(C_K, C_V).astype(BF16)
    qkv_ref[:, 1024:1536] = mm(C_V, C_POOL).astype(BF16)
    pool_ref[...] = mm(C_POOL, C_Z).astype(BF16)
    z_ref[...] = mm(C_Z, C_XBC).astype(BF16)
    xbc_ref[:, 0:512] = mm(C_XBC, C_XBC + 512).astype(BF16)
    xbc_ref[:, 512:768] = mm(C_XBC + 512, C_GATE).astype(BF16)
    for j in range(6):
        lo = C_GATE + 512 * j
        gate_ref[:, 512 * j:512 * (j + 1)] = _sigmoid(
            mm(lo, lo + 512) + bg_ref[:, 512 * j:512 * (j + 1)]).astype(BF16)
    dtx_ref[...] = _softplus(mm(C_DTX, C_DT8) + dtbx_ref[...])
    dt8 = _softplus(mm(C_DT8, C_END) + dtb8_ref[...])
    row = lax.broadcasted_iota(jnp.int32, dt8.shape, 0) % CHUNK
    cs8_ref[...] = _chunk_cumsum(dt8 * (-jnp.exp(alog8_ref[...])), row)


def _slot_specs(tm, n_steps):
    per = SLOT_BLOCK // tm
    return [pl.BlockSpec((SLOT_BLOCK,), lambda i: (i // per,), memory_space=pltpu.SMEM),
            pl.BlockSpec((SLOT_BLOCK,), lambda i: (jnp.minimum(i + 1, n_steps - 1) // per,),
                         memory_space=pltpu.SMEM)]


def _row_buffers(tm):
    return [pltpu.VMEM((2, tm * ROW_SUB, LANES), F32), pltpu.SemaphoreType.DMA((2,))]


def _inproj(x, g, w, bg, dtbx, dtb8, alog8, layer, moe=None):
    T, D = x.shape
    tm = TM_PROJ
    n_steps = T // tm
    const = lambda shape: pl.BlockSpec(shape, lambda i: (0, 0), pipeline_mode=pl.Buffered(1))
    row = lambda n: pl.BlockSpec((tm, n), lambda i: (i, 0))
    in_specs = [row(D), const((1, D)),
                pl.BlockSpec((None, D, C_END), lambda i: (layer, 0, 0),
                             pipeline_mode=pl.Buffered(1)),
                const((1, 3072)), const((1, 512)), const((1, LANES)), const((1, LANES))]
    out_specs = [row(1536), row(512), row(512), row(768), row(3072), row(512), row(LANES)]
    out_shape = [jax.ShapeDtypeStruct((T, 1536), BF16), jax.ShapeDtypeStruct((T, 512), BF16),
                 jax.ShapeDtypeStruct((T, 512), BF16), jax.ShapeDtypeStruct((T, 768), BF16),
                 jax.ShapeDtypeStruct((T, 3072), BF16), jax.ShapeDtypeStruct((T, 512), F32),
                 jax.ShapeDtypeStruct((T, LANES), F32)]
    args = (x, g, w, bg, dtbx, dtb8, alog8)
    scratch = []
    if moe is not None:
        slots, ys = moe
        in_specs = (_slot_specs(tm, n_steps) + [row(D), pl.BlockSpec(memory_space=pl.ANY)]
                    + in_specs[1:])
        out_specs = [row(D)] + out_specs
        out_shape = [jax.ShapeDtypeStruct((T, D), F32)] + out_shape
        args = (slots, slots, x, ys) + args[1:]
        scratch = _row_buffers(tm)
    return pl.pallas_call(
        functools.partial(_inproj_kernel, combine=moe is not None),
        grid=(n_steps,),
        in_specs=in_specs,
        out_specs=tuple(out_specs),
        out_shape=tuple(out_shape),
        scratch_shapes=scratch,
        compiler_params=_cparams("arbitrary" if moe is not None else "parallel"),
        name="inproj",
    )(*args)


def _attn_kernel(q_ref, k_ref, v_ref, bias_ref, lam_ref, g_ref, o_ref, vext_ref, *, out_scale):
    S = q_ref.shape[0]
    lam = lam_ref[...]
    g = g_ref[...]
    vext_ref[:, 0:A_V_DIM] = v_ref[...]
    vext_ref[:, A_V_DIM:2 * A_V_DIM] = jnp.ones((S, A_V_DIM), vext_ref.dtype)
    for i in range(S // TA):
        q = q_ref[i * TA:(i + 1) * TA, :]
        lane = lax.broadcasted_iota(jnp.int32, q.shape, 1)
        qq = jnp.concatenate([jnp.where(lane < A_QK_DIM, q, jnp.zeros_like(q)),
                              jnp.where(lane >= A_QK_DIM, q, jnp.zeros_like(q))], axis=0)
        pieces = []
        if i >= 2:
            pieces.append((0, (i - 1) * TA, None))
        if i >= 1:
            pieces.append(((i - 1) * TA, i * TA, 1))
        pieces.append((i * TA, (i + 1) * TA, 0))
        ss = []
        for lo, hi, bi in pieces:
            s = _dot_nt(qq, k_ref[lo:hi, :])
            if bi is not None:
                s = s + jnp.concatenate([bias_ref[bi], bias_ref[bi]], axis=0)
            ss.append(s)
        m = jnp.max(ss[0], axis=-1, keepdims=True)
        for s in ss[1:]:
            m = jnp.maximum(m, jnp.max(s, axis=-1, keepdims=True))
        acc = jnp.zeros((2 * TA, 2 * A_V_DIM), F32)
        for s, (lo, hi, _) in zip(ss, pieces):
            acc = acc + _dot(jnp.exp(s - m).astype(BF16), vext_ref[lo:hi, :])
        out = acc[:, 0:A_V_DIM] / acc[:, A_V_DIM:A_V_DIM + 1]
        o = out[0:TA] - lam * out[TA:2 * TA]
        o_ref[i * TA:(i + 1) * TA, :] = (_rms(o, g) * out_scale).astype(o_ref.dtype)


def _attention(qkv, bias, lam_row, subln_g, out_scale, B, S):
    qkv3 = qkv.reshape(B, S, 3 * A_WIDTH)
    seq = lambda off: pl.BlockSpec((None, S, LANES), lambda b, h: (b, 0, off + h))
    return pl.pallas_call(
        functools.partial(_attn_kernel, out_scale=out_scale),
        grid=(B, A_HEADS),
        in_specs=[seq(0), seq(A_HEADS), seq(2 * A_HEADS),
                  pl.BlockSpec((None, 2, TA, TA), lambda b, h: (h, 0, 0, 0)),
                  pl.BlockSpec((1, LANES), lambda b, h: (0, 0)),
                  pl.BlockSpec((1, LANES), lambda b, h: (0, 0))],
        out_specs=seq(0),
        out_shape=jax.ShapeDtypeStruct((B, S, A_WIDTH), BF16),
        scratch_shapes=[pltpu.VMEM((S, 2 * A_V_DIM), BF16)],
        compiler_params=_cparams("parallel", "parallel"),
        name="diff_attention",
    )(qkv3, qkv3, qkv3, bias, lam_row, subln_g)


def _pool_kernel(u_ref, w_ref, scale_ref, o_ref):
    S = u_ref.shape[0]
    row = lax.broadcasted_iota(jnp.int32, (S, POOL_GROUP_DIM), 0)
    t = (row + 1).astype(F32)
    for gi, win in enumerate(POOL_WINDOWS):
        sl = slice(gi * POOL_GROUP_DIM, (gi + 1) * POOL_GROUP_DIM)
        u = u_ref[:, sl].astype(F32)
        acc = u
        s = 1
        while s < win:
            acc = acc + _shift_rows(acc, s, row)
            s *= 2
        d = acc / jnp.minimum(t, float(win)) - u
        y = _dot(d.astype(BF16), w_ref[gi])
        o_ref[:, sl] = (y * scale_ref[:, sl]).astype(o_ref.dtype)


def _pool(u, w_group, scale, B, S):
    u3 = u.reshape(B, S, POOL_WIDTH)
    return pl.pallas_call(
        _pool_kernel,
        grid=(B,),
        in_specs=[pl.BlockSpec((None, S, POOL_WIDTH), lambda b: (b, 0, 0)),
                  pl.BlockSpec((len(POOL_WINDOWS), POOL_GROUP_DIM, POOL_GROUP_DIM),
                               lambda b: (0, 0, 0)),
                  pl.BlockSpec((1, POOL_WIDTH), lambda b: (0, 0))],
        out_specs=pl.BlockSpec((None, S, POOL_WIDTH), lambda b: (b, 0, 0)),
        out_shape=jax.ShapeDtypeStruct((B, S, POOL_WIDTH), BF16),
        compiler_params=_cparams("parallel"),
        name="pool_mixer",
    )(u3, w_group, scale)


def _ssd_kernel(xbc_ref, z_ref, dtx_ref, r_ref, convw_ref, convb_ref, alogx_ref, dx_ref, g_ref,
                o_ref, conv_ref, csx_ref, state_ref):
    S = xbc_ref.shape[0]
    nc = S // CHUNK
    GN = SSD_GROUPS * SSD_STATE
    HP = SSD_WIDTH

    xin = xbc_ref[...].astype(F32)
    row = lax.broadcasted_iota(jnp.int32, xin.shape, 0)
    acc = xin * convw_ref[SSD_CONV - 1:SSD_CONV, :] + convb_ref[...]
    for s in range(1, SSD_CONV):
        acc = acc + _shift_rows(xin, s, row) * convw_ref[SSD_CONV - 1 - s:SSD_CONV - s, :]
    conv_ref[...] = _silu(acc)

    row_c = lax.broadcasted_iota(jnp.int32, (S, HP), 0) % CHUNK
    csx_ref[...] = _chunk_cumsum(dtx_ref[...] * (-jnp.exp(alogx_ref[...])), row_c)

    state_ref[...] = jnp.zeros(state_ref.shape, F32)

    li = lax.broadcasted_iota(jnp.int32, (CHUNK, HP), 0)
    ci = lax.broadcasted_iota(jnp.int32, (CHUNK, HP), 1)
    tril = li >= (ci % CHUNK)
    br = lax.broadcasted_iota(jnp.int32, (HP, GN), 0)
    bc = lax.broadcasted_iota(jnp.int32, (HP, GN), 1)
    bmask = (br // (HP // SSD_GROUPS)) == (bc // SSD_STATE)
    sr = lax.broadcasted_iota(jnp.int32, (GN, HP), 0)
    sc = lax.broadcasted_iota(jnp.int32, (GN, HP), 1)
    smask = (sr // SSD_STATE) == (sc // (HP // SSD_GROUPS))
    GW = HP // SSD_GROUPS
    dr = lax.broadcasted_iota(jnp.int32, (GW, GW), 0)
    dc = lax.broadcasted_iota(jnp.int32, (GW, GW), 1)
    dmask = (dr // CHUNK) == (dc // SSD_HEAD_DIM)

    def chunk_body(c, carry):
        r0 = pl.multiple_of(c * CHUNK, CHUNK)
        rows = pl.ds(r0, CHUNK)
        xs = conv_ref[rows, 0:HP]
        Bc = conv_ref[rows, HP:HP + GN].astype(BF16)
        Cc = conv_ref[rows, HP + GN:HP + 2 * GN].astype(BF16)
        dtx = dtx_ref[rows, :]
        cs = csx_ref[rows, :]
        cs_row = r_ref[pl.ds(c, 1), :]
        cs_end = csx_ref[pl.ds(r0 + CHUNK - 1, 1), :]

        xdt = xs * dtx
        Lm = jnp.exp(jnp.where(tril, cs - cs_row, NEG_BIG))
        b_rep = jnp.where(bmask, jnp.concatenate([Bc] * SSD_HEADS, axis=0), jnp.zeros((), BF16))
        scores = _dot_nt(Cc, b_rep)
        G = (scores * Lm).astype(BF16)

        xdt_b = xdt.astype(BF16)
        y_parts = []
        for g in range(SSD_GROUPS):
            xg = xdt_b[:, g * GW:(g + 1) * GW]
            xbd = jnp.where(dmask, jnp.concatenate([xg] * (GW // CHUNK), axis=0),
                            jnp.zeros((), BF16))
            y_parts.append(_dot(G[:, g * GW:(g + 1) * GW], xbd))
        y = jnp.concatenate(y_parts, axis=1)

        st = state_ref[...]
        y = y + _dot(Cc, st.astype(BF16)) * jnp.exp(cs)
        xdec = (xdt * jnp.exp(cs_end - cs)).astype(BF16)
        upd = jnp.where(smask, _dot_tn(Bc, xdec), 0.0)
        state_ref[...] = st * jnp.exp(cs_end) + upd

        y = y + dx_ref[...] * xs
        y = y * _silu(z_ref[rows, :].astype(F32))
        o_ref[rows, :] = _rms(y, g_ref[...]).astype(o_ref.dtype)
        return carry

    lax.fori_loop(0, nc, chunk_body, 0, unroll=2)


def _ssd(xbc, z, dtx, r, conv_w, conv_b, alogx, dx, norm_g, B, S):
    nc = S // CHUNK
    b3 = lambda n: pl.BlockSpec((None, S, n), lambda b: (b, 0, 0))
    const = lambda shape: pl.BlockSpec(shape, lambda b: (0,) * len(shape))
    return pl.pallas_call(
        _ssd_kernel,
        grid=(B,),
        in_specs=[b3(SSD_CONV_CH), b3(SSD_WIDTH), b3(SSD_WIDTH),
                  pl.BlockSpec((None, nc, SSD_WIDTH), lambda b: (b, 0, 0)),
                  const((SSD_CONV, SSD_CONV_CH)), const((1, SSD_CONV_CH)),
                  const((1, SSD_WIDTH)), const((1, SSD_WIDTH)), const((1, SSD_WIDTH))],
        out_specs=b3(SSD_WIDTH),
        out_shape=jax.ShapeDtypeStruct((B, S, SSD_WIDTH), BF16),
        scratch_shapes=[pltpu.VMEM((S, SSD_CONV_CH), F32), pltpu.VMEM((S, SSD_WIDTH), F32),
                        pltpu.VMEM((SSD_GROUPS * SSD_STATE, SSD_WIDTH), F32)],
        compiler_params=_cparams("parallel"),
        name="ssd_mixer",
    )(xbc.reshape(B, S, SSD_CONV_CH), z.reshape(B, S, SSD_WIDTH), dtx.reshape(B, S, SSD_WIDTH),
      r, conv_w, conv_b, alogx, dx, norm_g)


def _memkv_kernel(mem_ref, g_ref, w_ref, kv_ref):
    kv_ref[...] = _dot(_rms(mem_ref[...], g_ref[...]).astype(BF16), w_ref[...]).astype(kv_ref.dtype)


def _memkv(mem, g_mem, w_xkv, layer):
    B, M, D = mem.shape
    return pl.pallas_call(
        _memkv_kernel,
        grid=(B,),
        in_specs=[pl.BlockSpec((None, M, D), lambda b: (b, 0, 0)),
                  pl.BlockSpec((1, D), lambda b: (0, 0)),
                  pl.BlockSpec((None, D, 2 * X_WIDTH), lambda b: (layer, 0, 0))],
        out_specs=pl.BlockSpec((None, M, 2 * X_WIDTH), lambda b: (b, 0, 0)),
        out_shape=jax.ShapeDtypeStruct((B, M, 2 * X_WIDTH), BF16),
        compiler_params=_cparams("parallel"),
        name="mem_kv",
    )(mem, g_mem, w_xkv)


def _route(lg, le):
    lane = lax.broadcasted_iota(jnp.int32, lg.shape, 1)
    big = jnp.int32(LANES)
    lgm = jnp.where(lane < N_EGROUPS, lg, NEG_BIG)
    gmax = jnp.max(lgm, axis=-1, keepdims=True)
    g_p = 1.0 / jnp.sum(jnp.exp(lgm - gmax), axis=-1, keepdims=True)
    g_idx = jnp.min(jnp.where(lgm == gmax, lane, big), axis=-1, keepdims=True)
    in_grp = (lane // EXPERTS_PER_GROUP == g_idx) & (lane < N_EXPERTS)
    e1 = jnp.where(in_grp, le, NEG_BIG)
    v1 = jnp.max(e1, axis=-1, keepdims=True)
    i1 = jnp.min(jnp.where(e1 == v1, lane, big), axis=-1, keepdims=True)
    e2 = jnp.where(lane == i1, NEG_BIG, e1)
    v2 = jnp.max(e2, axis=-1, keepdims=True)
    i2 = jnp.min(jnp.where(e2 == v2, lane, big), axis=-1, keepdims=True)
    t = jnp.exp(v2 - v1)
    w1 = g_p / (1.0 + t)
    w2 = g_p * t / (1.0 + t)
    return lane, i1, i2, w1, w2


def _merge_kernel(x_ref, oa_ref, ob_ref, oc_ref, gate_ref, wb_ref, wmix_ref, gx_ref, wq_ref,
                  kv_ref, wo_ref, gm_ref, wr_ref, br_ref, x2_ref, hp_ref, route_ref, cnt_ref):
    D = x_ref.shape[1]
    merged = gate_ref[:, 0:D].astype(F32) * _dot(oa_ref[...], wb_ref[0])
    merged = merged + gate_ref[:, D:2 * D].astype(F32) * _dot(ob_ref[...], wb_ref[1])
    merged = merged + gate_ref[:, 2 * D:3 * D].astype(F32) * _dot(oc_ref[...], wb_ref[2])
    x1 = x_ref[...] + _dot(merged.astype(BF16), wmix_ref[...])

    hq = _rms(x1, gx_ref[...]).astype(BF16)
    q = (_dot(hq, wq_ref[...]) * (X_HEAD_DIM ** -0.5)).astype(BF16)
    k = kv_ref[:, 0:X_WIDTH]
    v = kv_ref[:, X_WIDTH:2 * X_WIDTH]
    qlane = lax.broadcasted_iota(jnp.int32, q.shape, 1) // X_HEAD_DIM
    vlane = lax.broadcasted_iota(jnp.int32, v.shape, 1) // X_HEAD_DIM
    o = jnp.zeros(q.shape, F32)
    for hd in range(X_HEADS):
        s = _dot_nt(jnp.where(qlane == hd, q, jnp.zeros_like(q)), k)
        p = jnp.exp(s - jnp.max(s, axis=-1, keepdims=True))
        p = p / jnp.sum(p, axis=-1, keepdims=True)
        o = o + _dot(p.astype(BF16), jnp.where(vlane == hd, v, jnp.zeros_like(v)))
    x2 = x1 + _dot(o.astype(BF16), wo_ref[...])
    x2_ref[...] = x2

    hm = _rms(x2, gm_ref[...]).astype(BF16)
    logits = _dot(hm, wr_ref[...]) + br_ref[...]
    lane, i1, i2, w1, w2 = _route(logits[:, 0:LANES], logits[:, LANES:2 * LANES])

    ea = jnp.minimum(i1, i2)
    eb = jnp.maximum(i1, i2)
    wa = jnp.where(i1 < i2, w1, w2)
    wb = jnp.where(i1 < i2, w2, w1)
    la = ea % EXPERTS_PER_GROUP
    lb = eb % EXPERTS_PER_GROUP
    pair = jnp.where(la == 0, lb - 1, jnp.where(la == 1, lb + 1, 5))
    cls = (ea // EXPERTS_PER_GROUP) * N_PAIRS + pair

    @pl.when(pl.program_id(0) == 0)
    def _():
        cnt_ref[...] = jnp.zeros(cnt_ref.shape, F32)

    oh = (lane == cls).astype(F32)
    tm = oh.shape[0]
    tr = lax.broadcasted_iota(jnp.int32, (tm, tm), 0)
    tc = lax.broadcasted_iota(jnp.int32, (tm, tm), 1)
    before = jnp.where(tr > tc, 1.0, 0.0).astype(BF16)
    prior = _dot(before, oh.astype(BF16)) + cnt_ref[0:1, :]
    rank = jnp.sum(prior * oh, axis=-1, keepdims=True)
    cnt_ref[...] = cnt_ref[...] + jnp.sum(oh, axis=0, keepdims=True)
    route = jnp.where(lane == 0, cls.astype(F32), 0.0)
    route_ref[...] = jnp.where(lane == 1, rank, route)

    bits = lax.bitcast_convert_type(hm.astype(F32), jnp.uint32)
    packed = (bits[:, 0:D // 2] >> 16) | bits[:, D // 2:D]
    for s in range(H_SUB):
        hp_ref[pl.ds(s, tm, stride=ROW_SUB), :] = packed[:, s * LANES:(s + 1) * LANES]
    wrow = jnp.where(lane == 0, wa, jnp.where(lane == 1, wb, 0.0))
    hp_ref[pl.ds(H_SUB, tm, stride=ROW_SUB), :] = lax.bitcast_convert_type(wrow, jnp.uint32)
    for s in range(H_SUB + 1, ROW_SUB):
        hp_ref[pl.ds(s, tm, stride=ROW_SUB), :] = jnp.zeros((tm, LANES), jnp.uint32)


def _merge(x, oa, ob, oc, gates, wb, wmix, gx, wq, kv, wo, gm, wr, br, S, layer):
    T, D = x.shape
    tm = TM_MERGE
    per_b = S // tm
    row = lambda n: pl.BlockSpec((tm, n), lambda i: (i, 0))
    const = lambda shape: pl.BlockSpec(shape, lambda i: (0,) * len(shape),
                                       pipeline_mode=pl.Buffered(1))
    stack = lambda shape: pl.BlockSpec((None,) + shape, lambda i: (layer,) + (0,) * len(shape),
                                       pipeline_mode=pl.Buffered(1))
    return pl.pallas_call(
        _merge_kernel,
        grid=(T // tm,),
        in_specs=[row(D), row(512), row(512), row(512), row(3 * D),
                  stack((N_BRANCH, 512, D)), stack((D, D)), const((1, D)), stack((D, X_WIDTH)),
                  pl.BlockSpec((None, N_MEM, 2 * X_WIDTH), lambda i: (i // per_b, 0, 0)),
                  stack((X_WIDTH, D)), const((1, D)), stack((D, 2 * LANES)),
                  const((1, 2 * LANES))],
        out_specs=(row(D), pl.BlockSpec((tm * ROW_SUB, LANES), lambda i: (i, 0)), row(LANES),
                   pl.BlockSpec((8, LANES), lambda i: (0, 0))),
        out_shape=(jax.ShapeDtypeStruct((T, D), F32),
                   jax.ShapeDtypeStruct((T * ROW_SUB, LANES), jnp.uint32),
                   jax.ShapeDtypeStruct((T, LANES), F32), jax.ShapeDtypeStruct((8, LANES), F32)),
        compiler_params=_cparams("arbitrary"),
        name="merge_xattn_router",
    )(x, oa, ob, oc, gates, wb, wmix, gx, wq, kv, wo, gm, wr, br)


def _plan(route, cnt, T):
    cls = route[:, 0].astype(jnp.int32)
    rank = route[:, 1].astype(jnp.int32)
    counts = cnt[0, :N_CLASSES].astype(jnp.int32)
    padded = ((counts + TME - 1) // TME) * TME
    ends = jnp.cumsum(padded)
    off = ends - padded
    onehot = cls[:, None] == jnp.arange(N_CLASSES, dtype=jnp.int32)
    slots = jnp.sum(jnp.where(onehot, off, 0), axis=-1) + rank
    n_tiles = T // TME + N_CLASSES
    tile_start = jnp.arange(n_tiles, dtype=jnp.int32) * TME
    tile_cls = jnp.minimum(jnp.sum(tile_start[:, None] >= ends[None, :], axis=1), N_CLASSES - 1)
    pair = tile_cls % N_PAIRS
    group = tile_cls // N_PAIRS
    la = (pair >= 3).astype(jnp.int32) + (pair >= 5).astype(jnp.int32)
    lb = jnp.where(pair < 3, pair + 1, jnp.where(pair < 5, pair - 1, 3))
    tile_ea = (group * EXPERTS_PER_GROUP + la).astype(jnp.int32)
    tile_eb = (group * EXPERTS_PER_GROUP + lb).astype(jnp.int32)
    n_used = ends[-1] // TME
    tile_src = jnp.where(jnp.arange(n_tiles) < n_used, jnp.arange(n_tiles), 0).astype(jnp.int32)
    pad_tile = jnp.where(padded > 0, ends - TME, -1).astype(jnp.int32)
    return (slots.astype(jnp.int32), tile_ea, tile_eb, tile_src,
            n_used.reshape(1).astype(jnp.int32), pad_tile)


def _row_copy(src, dst, i, j, sem):
    return pltpu.make_async_copy(_row_tile(src, i), _row_tile(dst, j), sem)


def _scatter_kernel(slots_ref, pad_ref, nu_ref, hp_ref, hs_ref, zbuf_ref, sem, zsem):
    tm = hp_ref.shape[0] // ROW_SUB

    def zero_tile(row):
        start = pl.multiple_of(row * ROW_SUB, TME * ROW_SUB)
        return pltpu.make_async_copy(zbuf_ref, hs_ref.at[pl.ds(start, TME * ROW_SUB)], zsem)

    @pl.when(pl.program_id(0) == 0)
    def _():
        zbuf_ref[...] = jnp.zeros(zbuf_ref.shape, zbuf_ref.dtype)
        for e in range(N_CLASSES):
            @pl.when(pad_ref[e] >= 0)
            def _():
                zero_tile(pad_ref[e]).start()
        for e in range(N_CLASSES):
            @pl.when(pad_ref[e] >= 0)
            def _():
                zero_tile(pad_ref[e]).wait()

        def tail(t, c):
            zero_tile(t * TME).start()
            zero_tile(t * TME).wait()
            return c

        lax.fori_loop(nu_ref[0], hs_ref.shape[0] // (TME * ROW_SUB), tail, 0)

    def start(r2, c):
        for k in range(2):
            _row_copy(hp_ref, hs_ref, 2 * r2 + k, slots_ref[2 * r2 + k], sem).start(priority=k)
        return c

    def wait(r2, c):
        for k in range(2):
            _row_copy(hp_ref, hs_ref, 2 * r2 + k, slots_ref[2 * r2 + k], sem).wait()
        return c

    lax.fori_loop(0, tm // 2, start, 0, unroll=8)
    lax.fori_loop(0, tm // 2, wait, 0, unroll=8)


def _scatter(slots, pad_tile, n_used, hp, n_rows):
    T = hp.shape[0] // ROW_SUB
    tm = TM_SCATTER
    return pl.pallas_call(
        _scatter_kernel,
        grid=(T // tm,),
        in_specs=[pl.BlockSpec((tm,), lambda i: (i,), memory_space=pltpu.SMEM),
                  pl.BlockSpec(memory_space=pltpu.SMEM),
                  pl.BlockSpec(memory_space=pltpu.SMEM),
                  pl.BlockSpec((tm * ROW_SUB, LANES), lambda i: (i, 0))],
        out_specs=pl.BlockSpec(memory_space=pl.ANY),
        out_shape=jax.ShapeDtypeStruct((n_rows * ROW_SUB, LANES), hp.dtype),
        scratch_shapes=[pltpu.VMEM((TME * ROW_SUB, LANES), hp.dtype), pltpu.SemaphoreType.DMA(()),
                        pltpu.SemaphoreType.DMA(())],
        compiler_params=_cparams("arbitrary"),
        name="moe_scatter",
    )(slots, pad_tile, n_used, hp)


def _expert_kernel(ea_ref, eb_ref, ts_ref, nu_ref, hs_ref, wga_ref, wua_ref, wda_ref,
                   wgb_ref, wub_ref, wdb_ref, ys_ref, *cache):
    i = pl.program_id(0)
    active = i < nu_ref[0]
    prev = jnp.maximum(i - 1, 0)

    for e_ref, srcs, dsts in ((ea_ref, (wga_ref, wua_ref, wda_ref), cache[0:3]),
                              (eb_ref, (wgb_ref, wub_ref, wdb_ref), cache[3:6])):
        @pl.when(active & ((i == 0) | (e_ref[i] != e_ref[prev])))
        def _():
            for src, dst in zip(srcs, dsts):
                dst[...] = src[...].astype(BF16)

    @pl.when(active)
    def _():
        words = [hs_ref[pl.ds(s, TME, stride=ROW_SUB), :] for s in range(H_SUB)]
        low = [lax.bitcast_convert_type(u << 16, F32) for u in words]
        high = [lax.bitcast_convert_type(u & jnp.uint32(0xFFFF0000), F32) for u in words]
        h = jnp.concatenate(low + high, axis=1).astype(BF16)
        wts = lax.bitcast_convert_type(hs_ref[pl.ds(H_SUB, TME, stride=ROW_SUB), :], F32)
        y = None
        for k in range(2):
            wg, wu, wd = cache[3 * k:3 * k + 3]
            a = _silu(_dot(h, wg[...])) * _dot(h, wu[...])
            yk = wts[:, k:k + 1] * _dot(a.astype(BF16), wd[...])
            y = yk if y is None else y + yk
        _store_rows(ys_ref, y)

    @pl.when(i >= nu_ref[0])
    def _():
        ys_ref[...] = jnp.zeros(ys_ref.shape, ys_ref.dtype)


def _experts(tile_ea, tile_eb, tile_src, n_used, hs, wg, wu, wd, layer):
    P = hs.shape[0] // ROW_SUB
    D = ROW_SUB * LANES

    def wspec(which, a, b):
        return pl.BlockSpec((None, None, a, b),
                            lambda i, ea, eb, ts, nu: (layer, (ea, eb)[which][i], 0, 0))

    shapes = ((D, EXPERT_FF), (D, EXPERT_FF), (EXPERT_FF, D))
    grid_spec = pltpu.PrefetchScalarGridSpec(
        num_scalar_prefetch=4,
        grid=(P // TME,),
        in_specs=([pl.BlockSpec((TME * ROW_SUB, LANES), lambda i, ea, eb, ts, nu: (ts[i], 0))]
                  + [wspec(0, *s) for s in shapes] + [wspec(1, *s) for s in shapes]),
        out_specs=pl.BlockSpec((TME * ROW_SUB, LANES), lambda i, ea, eb, ts, nu: (i, 0)),
        scratch_shapes=[pltpu.VMEM(s, BF16) for s in shapes + shapes])
    return pl.pallas_call(
        _expert_kernel,
        grid_spec=grid_spec,
        out_shape=jax.ShapeDtypeStruct((P * ROW_SUB, LANES), F32),
        compiler_params=_cparams("arbitrary"),
        name="moe_experts",
    )(tile_ea, tile_eb, tile_src, n_used, hs, wg, wu, wd, wg, wu, wd)


def _final_kernel(slots_ref, next_slots_ref, x2_ref, ys_ref, gf_ref, o_ref, ybuf_ref, sem):
    y = _combined_residual(slots_ref, next_slots_ref, x2_ref, ys_ref, ybuf_ref, sem,
                           x2_ref.shape[0])
    o_ref[...] = _rms(y, gf_ref[...])


def _final(slots, x2, ys, g_final):
    T, D = x2.shape
    tm = TM_MOE
    n_steps = T // tm
    row = lambda n: pl.BlockSpec((tm, n), lambda i: (i, 0))
    return pl.pallas_call(
        _final_kernel,
        grid=(n_steps,),
        in_specs=_slot_specs(tm, n_steps) + [row(D), pl.BlockSpec(memory_space=pl.ANY),
                                             pl.BlockSpec((1, D), lambda i: (0, 0))],
        out_specs=row(D),
        out_shape=jax.ShapeDtypeStruct((T, D), F32),
        scratch_shapes=_row_buffers(tm),
        compiler_params=_cparams("arbitrary"),
        name="moe_combine_final",
    )(slots, slots, x2, ys, g_final)


def _prep_inproj_weight(w_in):
    col_dt = 2560 + SSD_CONV_CH
    w_dt = w_in[:, :, col_dt:col_dt + SSD_HEADS]
    return jnp.concatenate(
        [w_in[:, :, :col_dt], w_in[:, :, col_dt + SSD_HEADS:],
         jnp.repeat(w_dt, SSD_HEAD_DIM, axis=2),
         jnp.pad(w_dt, ((0, 0), (0, 0), (0, LANES - SSD_HEADS)))], axis=2).astype(BF16)


def _pad_lanes(v, n=LANES):
    return jnp.pad(v, (0, n - v.shape[0]))[None, :]


def kernel(x, mem, w_in, b_gate, lam_q1, lam_k1, lam_q2, lam_k2, attn_subln_g, rel_bias_table,
           w_pool_group, pool_scale, ssd_conv_w, ssd_conv_b, ssd_dt_bias, ssd_A_log, ssd_D,
           ssd_norm_g, w_branch, w_mix_out, g_mix, g_xattn, g_mem, w_xq, w_xkv, w_xo, g_moe,
           w_router_group, b_router_group, w_router_expert, b_router_expert,
           w_exp_gate, w_exp_up, w_exp_down, g_final):
    B, S, D = x.shape
    T = B * S
    depth = w_in.shape[0]
    nc = S // CHUNK
    lam_inits = tuple(0.8 - 0.6 * math.exp(-0.3 * l) for l in range(depth))
    bias, lam = _setup(rel_bias_table, lam_q1, lam_k1, lam_q2, lam_k2, lam_inits)

    xf = x.reshape(T, D)
    w_proj = _prep_inproj_weight(w_in)
    w_branch_b, w_mix_b = w_branch.astype(BF16), w_mix_out.astype(BF16)
    w_xq_b, w_xkv_b, w_xo_b = w_xq.astype(BF16), w_xkv.astype(BF16), w_xo.astype(BF16)
    w_r = jnp.concatenate(
        [jnp.pad(w_router_group, ((0, 0), (0, 0), (0, LANES - N_EGROUPS))),
         jnp.pad(w_router_expert, ((0, 0), (0, 0), (0, LANES - N_EXPERTS)))], axis=2).astype(BF16)
    moe = None
    for l in range(depth):
        outs = _inproj(
            xf, g_mix[l][None, :], w_proj, b_gate[l][None, :],
            jnp.repeat(ssd_dt_bias[l], SSD_HEAD_DIM)[None, :], _pad_lanes(ssd_dt_bias[l]),
            _pad_lanes(ssd_A_log[l]), l, moe)
        if moe is not None:
            xf, outs = outs[0], outs[1:]
        qkv, pool_u, z, xbc, gates, dtx, cs8 = outs
        o_a = _attention(qkv, bias, lam[l][None, :], attn_subln_g[l][None, :],
                         1.0 - lam_inits[l], B, S)
        o_b = _pool(pool_u, w_pool_group[l].astype(BF16), pool_scale[l][None, :], B, S)
        r = cs8[:, :SSD_HEADS].reshape(B, nc, CHUNK, SSD_HEADS).transpose(0, 1, 3, 2)
        r = r.reshape(B, nc, SSD_WIDTH)
        o_c = _ssd(xbc, z, dtx, r, ssd_conv_w[l], ssd_conv_b[l][None, :],
                   jnp.repeat(ssd_A_log[l], SSD_HEAD_DIM)[None, :],
                   jnp.repeat(ssd_D[l], SSD_HEAD_DIM)[None, :], ssd_norm_g[l][None, :], B, S)
        kv = _memkv(mem, g_mem[l][None, :], w_xkv_b, l)
        b_r = jnp.concatenate([_pad_lanes(b_router_group[l]), _pad_lanes(b_router_expert[l])],
                              axis=1)
        x2, hp, route, cnt = _merge(
            xf, o_a.reshape(T, A_WIDTH), o_b.reshape(T, POOL_WIDTH), o_c.reshape(T, SSD_WIDTH),
            gates, w_branch_b, w_mix_b, g_xattn[l][None, :], w_xq_b, kv, w_xo_b,
            g_moe[l][None, :], w_r, b_r, S, l)
        slots, tile_ea, tile_eb, tile_src, n_used, pad_tile = _plan(route, cnt, T)
        hs = _scatter(slots, pad_tile, n_used, hp, T + N_CLASSES * TME)
        ys = _experts(tile_ea, tile_eb, tile_src, n_used, hs, w_exp_gate, w_exp_up, w_exp_down, l)
        xf, moe = x2, (slots, ys)
    return _final(moe[0], xf, moe[1], g_final[None, :]).reshape(B, S, D)
```

```python
import functools
import math

import jax
import jax.numpy as jnp
from jax import lax
from jax.experimental import pallas as pl
from jax.experimental.pallas import tpu as pltpu

F32 = jnp.float32
BF16 = jnp.bfloat16

EPS = 1e-6
CHUNK = 64
N_MEM = 256

A_HEADS = 4
A_QK_DIM = 64
A_V_DIM = 128
A_WIDTH = 512
REL_BUCKETS = 32
REL_MAX_DIST = 128

POOL_WINDOWS = (2, 4, 8, 16)
POOL_GROUP_DIM = 128
POOL_WIDTH = 512

SSD_HEADS = 8
SSD_HEAD_DIM = 64
SSD_WIDTH = 512
SSD_GROUPS = 2
SSD_STATE = 64
SSD_CONV = 4
SSD_CONV_CH = 768

N_BRANCH = 3
X_HEADS = 4
X_HEAD_DIM = 64
X_WIDTH = 256

N_EGROUPS = 4
EXPERTS_PER_GROUP = 4
N_EXPERTS = 16
EXPERT_FF = 512

LANES = 128
NEG_BIG = -1e30

C_Q, C_K, C_V, C_POOL, C_Z, C_XBC, C_GATE = 0, 512, 1024, 1536, 2048, 2560, 3328
C_DTX = 6400
C_DT8 = 6912
C_END = 7040

TM_PROJ = 512
TM_MERGE = 512
TM_MOE = 512
TM_SCATTER = 1024
SLOT_BLOCK = 1024
TME = 256
N_PAIRS = 6
N_CLASSES = N_EGROUPS * N_PAIRS
H_SUB = 4
TA = 256
VMEM_LIMIT = 56 * 1024 * 1024


def _cparams(*sem):
    return pltpu.CompilerParams(dimension_semantics=sem, vmem_limit_bytes=VMEM_LIMIT)


def _rms(x, g):
    return x * lax.rsqrt(jnp.mean(x * x, axis=-1, keepdims=True) + EPS) * g


def _dot(a, b):
    return jnp.dot(a, b, preferred_element_type=F32)


def _dot_nt(a, b):
    return lax.dot_general(a, b, (((1,), (1,)), ((), ())), preferred_element_type=F32)


def _dot_tn(a, b):
    return lax.dot_general(a, b, (((0,), (0,)), ((), ())), preferred_element_type=F32)


def _sigmoid(x):
    return 1.0 / (1.0 + jnp.exp(-x))


def _silu(x):
    return x * _sigmoid(x)


def _softplus(x):
    return jnp.maximum(x, 0.0) + jnp.log(1.0 + jnp.exp(-jnp.abs(x)))


ROW_SUB = 8


def _store_rows(ref, x):
    m = x.shape[0]
    for s in range(ROW_SUB):
        ref[pl.ds(s, m, stride=ROW_SUB), :] = x[:, s * LANES:(s + 1) * LANES]


def _load_rows(ref):
    m = ref.shape[0] // ROW_SUB
    return jnp.concatenate([ref[pl.ds(s, m, stride=ROW_SUB), :] for s in range(ROW_SUB)], axis=1)


def _row_tile(ref, r):
    return ref.at[pl.ds(pl.multiple_of(r * ROW_SUB, ROW_SUB), ROW_SUB)]


def _shift_rows(v, s, row):
    return jnp.where(row >= s, pltpu.roll(v, s, axis=0), 0.0)


def _chunk_cumsum(v, row_in_chunk):
    s = 1
    while s < CHUNK:
        v = v + jnp.where(row_in_chunk >= s, pltpu.roll(v, s, axis=0), 0.0)
        s *= 2
    return v


def _setup_kernel(table_ref, bucket_ref, vis_ref, lq1_ref, lk1_ref, lq2_ref, lk2_ref,
                  bias_ref, lam_ref, *, lam_inits):
    for h in range(A_HEADS):
        tiles = []
        for d in range(3):
            bkt = bucket_ref[d]
            acc = jnp.zeros(bkt.shape, F32)
            for b in range(REL_BUCKETS):
                acc = acc + jnp.where(bkt == b, table_ref[b, h], 0.0)
            tiles.append(acc)
        bias_ref[h, 0] = jnp.where(vis_ref[...] > 0, tiles[0] - tiles[2], NEG_BIG)
        bias_ref[h, 1] = tiles[1] - tiles[2]
    s1 = jnp.sum(lq1_ref[...] * lk1_ref[...], axis=-1, keepdims=True)
    s2 = jnp.sum(lq2_ref[...] * lk2_ref[...], axis=-1, keepdims=True)
    row = lax.broadcasted_iota(jnp.int32, s1.shape, 0)
    init = jnp.zeros(s1.shape, F32)
    for l, v in enumerate(lam_inits):
        init = jnp.where(row == l, v, init)
    lam_ref[...] = jnp.broadcast_to(jnp.exp(s1) - jnp.exp(s2) + init, lam_ref.shape)


def _rel_bucket_index(rel):
    nb = REL_BUCKETS // 2
    max_exact = nb // 2
    side = jnp.where(rel > 0, nb, 0)
    n = jnp.abs(rel)
    n_f = jnp.maximum(n, 1).astype(F32)
    large = max_exact + (jnp.log(n_f / max_exact) / math.log(REL_MAX_DIST / max_exact)
                         * (nb - max_exact)).astype(jnp.int32)
    large = jnp.minimum(large, nb - 1)
    return side + jnp.where(n < max_exact, n, large)


def _setup(rel_bias_table, lam_q1, lam_k1, lam_q2, lam_k2, lam_inits):
    depth = lam_q1.shape[0]
    r = jnp.arange(TA)
    rel0 = r[None, :] - r[:, None]
    bucket = jnp.stack([_rel_bucket_index(rel0 - d * TA) for d in range(3)]).astype(jnp.int32)
    vis = ((r[None, :] // CHUNK) <= (r[:, None] // CHUNK)).astype(jnp.int32)
    vm = pl.BlockSpec(memory_space=pltpu.VMEM)
    return pl.pallas_call(
        functools.partial(_setup_kernel, lam_inits=lam_inits),
        out_shape=(jax.ShapeDtypeStruct((A_HEADS, 2, TA, TA), F32),
                   jax.ShapeDtypeStruct((depth, LANES), F32)),
        in_specs=[pl.BlockSpec(memory_space=pltpu.SMEM), vm, vm, vm, vm, vm, vm],
        out_specs=(vm, vm),
        name="setup_bias_lam",
    )(rel_bias_table, bucket, vis, lam_q1, lam_k1, lam_q2, lam_k2)


def _gather_rows(slots_ref, base, ys_ref, ybuf_ref, sem, tm, wait):
    def body(r2, c):
        for k in range(2):
            r = 2 * r2 + k
            cp = pltpu.make_async_copy(_row_tile(ys_ref, slots_ref[base + r]),
                                       _row_tile(ybuf_ref, r), sem)
            if wait:
                cp.wait()
            else:
                cp.start(priority=k)
        return c

    lax.fori_loop(0, tm // 2, body, 0, unroll=8)


def _combined_residual(slots_ref, next_slots_ref, x2_ref, ys_ref, ybuf_ref, sem, tm):
    i = pl.program_id(0)
    cur = i % 2
    per = SLOT_BLOCK // tm
    base = (i % per) * tm
    next_base = ((i + 1) % per) * tm

    @pl.when(i == 0)
    def _():
        _gather_rows(slots_ref, base, ys_ref, ybuf_ref.at[0], sem.at[0], tm, wait=False)

    @pl.when(i + 1 < pl.num_programs(0))
    def _():
        _gather_rows(next_slots_ref, next_base, ys_ref, ybuf_ref.at[1 - cur], sem.at[1 - cur], tm,
                     wait=False)

    _gather_rows(slots_ref, base, ys_ref, ybuf_ref.at[cur], sem.at[cur], tm, wait=True)
    return x2_ref[...] + _load_rows(ybuf_ref.at[cur])


def _inproj_kernel(*refs, combine):
    if combine:
        (slots_ref, next_slots_ref, x2_ref, ys_ref, g_ref, w_ref, bg_ref, dtbx_ref,
         dtb8_ref, alog8_ref, x_ref, qkv_ref, pool_ref, z_ref, xbc_ref, gate_ref, dtx_ref,
         cs8_ref, ybuf_ref, sem) = refs
        x = _combined_residual(slots_ref, next_slots_ref, x2_ref, ys_ref, ybuf_ref, sem,
                               x2_ref.shape[0])
        x_ref[...] = x
    else:
        (x_ref, g_ref, w_ref, bg_ref, dtbx_ref, dtb8_ref, alog8_ref,
         qkv_ref, pool_ref, z_ref, xbc_ref, gate_ref, dtx_ref, cs8_ref) = refs
        x = x_ref[...]
    h = _rms(x, g_ref[...]).astype(BF16)

    def mm(lo, hi):
        return _dot(h, w_ref[:, lo:hi])

    qkv_ref[:, 0:512] = (mm(C_Q, C_K) * (A_QK_DIM ** -0.5)).astype(BF16)
    qkv_ref[:, 512:1024] = mm(C_K, C_V).astype(BF16)
    qkv_ref[:, 1024:1536] = mm(C_V, C_POOL).astype(BF16)
    pool_ref[...] = mm(C_POOL, C_Z).astype(BF16)
    z_ref[...] = mm(C_Z, C_XBC).astype(BF16)
    xbc_ref[:, 0:512] = mm(C_XBC, C_XBC + 512).astype(BF16)
    xbc_ref[:, 512:768] = mm(C_XBC + 512, C_GATE).astype(BF16)
    for j in range(6):
        lo = C_GATE + 512 * j
        gate_ref[:, 512 * j:512 * (j + 1)] = _sigmoid(
            mm(lo, lo + 512) + bg_ref[:, 512 * j:512 * (j + 1)]).astype(BF16)
    dtx_ref[...] = _softplus(mm(C_DTX, C_DT8) + dtbx_ref[...])
    dt8 = _softplus(mm(C_DT8, C_END) + dtb8_ref[...])
    row = lax.broadcasted_iota(jnp.int32, dt8.shape, 0) % CHUNK
    cs8_ref[...] = _chunk_cumsum(dt8 * (-jnp.exp(alog8_ref[...])), row)


def _slot_specs(tm, n_steps):
    per = SLOT_BLOCK // tm
    return [pl.BlockSpec((SLOT_BLOCK,), lambda i: (i // per,), memory_space=pltpu.SMEM),
            pl.BlockSpec((SLOT_BLOCK,), lambda i: (jnp.minimum(i + 1, n_steps - 1) // per,),
                         memory_space=pltpu.SMEM)]


def _row_buffers(tm):
    return [pltpu.VMEM((2, tm * ROW_SUB, LANES), F32), pltpu.SemaphoreType.DMA((2,))]


def _inproj(x, g, w, bg, dtbx, dtb8, alog8, layer, moe=None):
    T, D = x.shape
    tm = TM_PROJ
    n_steps = T // tm
    const = lambda shape: pl.BlockSpec(shape, lambda i: (0, 0), pipeline_mode=pl.Buffered(1))
    row = lambda n: pl.BlockSpec((tm, n), lambda i: (i, 0))
    in_specs = [row(D), const((1, D)),
                pl.BlockSpec((None, D, C_END), lambda i: (layer, 0, 0),
                             pipeline_mode=pl.Buffered(1)),
                const((1, 3072)), const((1, 512)), const((1, LANES)), const((1, LANES))]
    out_specs = [row(1536), row(512), row(512), row(768), row(3072), row(512), row(LANES)]
    out_shape = [jax.ShapeDtypeStruct((T, 1536), BF16), jax.ShapeDtypeStruct((T, 512), BF16),
                 jax.ShapeDtypeStruct((T, 512), BF16), jax.ShapeDtypeStruct((T, 768), BF16),
                 jax.ShapeDtypeStruct((T, 3072), BF16), jax.ShapeDtypeStruct((T, 512), F32),
                 jax.ShapeDtypeStruct((T, LANES), F32)]
    args = (x, g, w, bg, dtbx, dtb8, alog8)
    scratch = []
    if moe is not None:
        slots, ys = moe
        in_specs = (_slot_specs(tm, n_steps) + [row(D), pl.BlockSpec(memory_space=pl.ANY)]
                    + in_specs[1:])
        out_specs = [row(D)] + out_specs
        out_shape = [jax.ShapeDtypeStruct((T, D), F32)] + out_shape
        args = (slots, slots, x, ys) + args[1:]
        scratch = _row_buffers(tm)
    return pl.pallas_call(
        functools.partial(_inproj_kernel, combine=moe is not None),
        grid=(n_steps,),
        in_specs=in_specs,
        out_specs=tuple(out_specs),
        out_shape=tuple(out_shape),
        scratch_shapes=scratch,
        compiler_params=_cparams("arbitrary" if moe is not None else "parallel"),
        name="inproj",
    )(*args)


def _attn_kernel(q_ref, k_ref, v_ref, bias_ref, lam_ref, g_ref, o_ref, vext_ref, *, out_scale):
    S = q_ref.shape[0]
    lam = lam_ref[...]
    g = g_ref[...]
    vext_ref[:, 0:A_V_DIM] = v_ref[...]
    vext_ref[:, A_V_DIM:2 * A_V_DIM] = jnp.ones((S, A_V_DIM), vext_ref.dtype)
    for i in range(S // TA):
        q = q_ref[i * TA:(i + 1) * TA, :]
        lane = lax.broadcasted_iota(jnp.int32, q.shape, 1)
        qq = jnp.concatenate([jnp.where(lane < A_QK_DIM, q, jnp.zeros_like(q)),
                              jnp.where(lane >= A_QK_DIM, q, jnp.zeros_like(q))], axis=0)
        pieces = []
        if i >= 2:
            pieces.append((0, (i - 1) * TA, None))
        if i >= 1:
            pieces.append(((i - 1) * TA, i * TA, 1))
        pieces.append((i * TA, (i + 1) * TA, 0))
        ss = []
        for lo, hi, bi in pieces:
            s = _dot_nt(qq, k_ref[lo:hi, :])
            if bi is not None:
                s = s + jnp.concatenate([bias_ref[bi], bias_ref[bi]], axis=0)
            ss.append(s)
        m = jnp.max(ss[0], axis=-1, keepdims=True)
        for s in ss[1:]:
            m = jnp.maximum(m, jnp.max(s, axis=-1, keepdims=True))
        acc = jnp.zeros((2 * TA, 2 * A_V_DIM), F32)
        for s, (lo, hi, _) in zip(ss, pieces):
            acc = acc + _dot(jnp.exp(s - m).astype(BF16), vext_ref[lo:hi, :])
        out = acc[:, 0:A_V_DIM] / acc[:, A_V_DIM:A_V_DIM + 1]
        o = out[0:TA] - lam * out[TA:2 * TA]
        o_ref[i * TA:(i + 1) * TA, :] = (_rms(o, g) * out_scale).astype(o_ref.dtype)


def _attention(qkv, bias, lam_row, subln_g, out_scale, B, S):
    qkv3 = qkv.reshape(B, S, 3 * A_WIDTH)
    seq = lambda off: pl.BlockSpec((None, S, LANES), lambda b, h: (b, 0, off + h))
    return pl.pallas_call(
        functools.partial(_attn_kernel, out_scale=out_scale),
        grid=(B, A_HEADS),
        in_specs=[seq(0), seq(A_HEADS), seq(2 * A_HEADS),
                  pl.BlockSpec((None, 2, TA, TA), lambda b, h: (h, 0, 0, 0)),
                  pl.BlockSpec((1, LANES), lambda b, h: (0, 0)),
                  pl.BlockSpec((1, LANES), lambda b, h: (0, 0))],
        out_specs=seq(0),
        out_shape=jax.ShapeDtypeStruct((B, S, A_WIDTH), BF16),
        scratch_shapes=[pltpu.VMEM((S, 2 * A_V_DIM), BF16)],
        compiler_params=_cparams("parallel", "parallel"),
        name="diff_attention",
    )(qkv3, qkv3, qkv3, bias, lam_row, subln_g)


def _pool_kernel(u_ref, w_ref, scale_ref, o_ref):
    S = u_ref.shape[0]
    row = lax.broadcasted_iota(jnp.int32, (S, POOL_GROUP_DIM), 0)
    t = (row + 1).astype(F32)
    for gi, win in enumerate(POOL_WINDOWS):
        sl = slice(gi * POOL_GROUP_DIM, (gi + 1) * POOL_GROUP_DIM)
        u = u_ref[:, sl].astype(F32)
        acc = u
        s = 1
        while s < win:
            acc = acc + _shift_rows(acc, s, row)
            s *= 2
        d = acc / jnp.minimum(t, float(win)) - u
        y = _dot(d.astype(BF16), w_ref[gi])
        o_ref[:, sl] = (y * scale_ref[:, sl]).astype(o_ref.dtype)


def _pool(u, w_group, scale, B, S):
    u3 = u.reshape(B, S, POOL_WIDTH)
    return pl.pallas_call(
        _pool_kernel,
        grid=(B,),
        in_specs=[pl.BlockSpec((None, S, POOL_WIDTH), lambda b: (b, 0, 0)),
                  pl.BlockSpec((len(POOL_WINDOWS), POOL_GROUP_DIM, POOL_GROUP_DIM),
                               lambda b: (0, 0, 0)),
                  pl.BlockSpec((1, POOL_WIDTH), lambda b: (0, 0))],
        out_specs=pl.BlockSpec((None, S, POOL_WIDTH), lambda b: (b, 0, 0)),
        out_shape=jax.ShapeDtypeStruct((B, S, POOL_WIDTH), BF16),
        compiler_params=_cparams("parallel"),
        name="pool_mixer",
    )(u3, w_group, scale)


def _ssd_kernel(xbc_ref, z_ref, dtx_ref, r_ref, convw_ref, convb_ref, alogx_ref, dx_ref, g_ref,
                o_ref, conv_ref, csx_ref, state_ref):
    S = xbc_ref.shape[0]
    nc = S // CHUNK
    GN = SSD_GROUPS * SSD_STATE
    HP = SSD_WIDTH

    xin = xbc_ref[...].astype(F32)
    row = lax.broadcasted_iota(jnp.int32, xin.shape, 0)
    acc = xin * convw_ref[SSD_CONV - 1:SSD_CONV, :] + convb_ref[...]
    for s in range(1, SSD_CONV):
        acc = acc + _shift_rows(xin, s, row) * convw_ref[SSD_CONV - 1 - s:SSD_CONV - s, :]
    conv_ref[...] = _silu(acc)

    row_c = lax.broadcasted_iota(jnp.int32, (S, HP), 0) % CHUNK
    csx_ref[...] = _chunk_cumsum(dtx_ref[...] * (-jnp.exp(alogx_ref[...])), row_c)

    state_ref[...] = jnp.zeros(state_ref.shape, F32)

    li = lax.broadcasted_iota(jnp.int32, (CHUNK, HP), 0)
    ci = lax.broadcasted_iota(jnp.int32, (CHUNK, HP), 1)
    tril = li >= (ci % CHUNK)
    br = lax.broadcasted_iota(jnp.int32, (HP, GN), 0)
    bc = lax.broadcasted_iota(jnp.int32, (HP, GN), 1)
    bmask = (br // (HP // SSD_GROUPS)) == (bc // SSD_STATE)
    sr = lax.broadcasted_iota(jnp.int32, (GN, HP), 0)
    sc = lax.broadcasted_iota(jnp.int32, (GN, HP), 1)
    smask = (sr // SSD_STATE) == (sc // (HP // SSD_GROUPS))
    GW = HP // SSD_GROUPS
    dr = lax.broadcasted_iota(jnp.int32, (GW, GW), 0)
    dc = lax.broadcasted_iota(jnp.int32, (GW, GW), 1)
    dmask = (dr // CHUNK) == (dc // SSD_HEAD_DIM)

    def chunk_body(c, carry):
        r0 = pl.multiple_of(c * CHUNK, CHUNK)
        rows = pl.ds(r0, CHUNK)
        xs = conv_ref[rows, 0:HP]
        Bc = conv_ref[rows, HP:HP + GN].astype(BF16)
        Cc = conv_ref[rows, HP + GN:HP + 2 * GN].astype(BF16)
        dtx = dtx_ref[rows, :]
        cs = csx_ref[rows, :]
        cs_row = r_ref[pl.ds(c, 1), :]
        cs_end = csx_ref[pl.ds(r0 + CHUNK - 1, 1), :]

        xdt = xs * dtx
        Lm = jnp.exp(jnp.where(tril, cs - cs_row, NEG_BIG))
        b_rep = jnp.where(bmask, jnp.concatenate([Bc] * SSD_HEADS, axis=0), jnp.zeros((), BF16))
        scores = _dot_nt(Cc, b_rep)
        G = (scores * Lm).astype(BF16)

        xdt_b = xdt.astype(BF16)
        y_parts = []
        for g in range(SSD_GROUPS):
            xg = xdt_b[:, g * GW:(g + 1) * GW]
            xbd = jnp.where(dmask, jnp.concatenate([xg] * (GW // CHUNK), axis=0),
                            jnp.zeros((), BF16))
            y_parts.append(_dot(G[:, g * GW:(g + 1) * GW], xbd))
        y = jnp.concatenate(y_parts, axis=1)

        st = state_ref[...]
        y = y + _dot(Cc, st.astype(BF16)) * jnp.exp(cs)
        xdec = (xdt * jnp.exp(cs_end - cs)).astype(BF16)
        upd = jnp.where(smask, _dot_tn(Bc, xdec), 0.0)
        state_ref[...] = st * jnp.exp(cs_end) + upd

        y = y + dx_ref[...] * xs
        y = y * _silu(z_ref[rows, :].astype(F32))
        o_ref[rows, :] = _rms(y, g_ref[...]).astype(o_ref.dtype)
        return carry

    lax.fori_loop(0, nc, chunk_body, 0, unroll=4)


def _ssd(xbc, z, dtx, r, conv_w, conv_b, alogx, dx, norm_g, B, S):
    nc = S // CHUNK
    b3 = lambda n: pl.BlockSpec((None, S, n), lambda b: (b, 0, 0))
    const = lambda shape: pl.BlockSpec(shape, lambda b: (0,) * len(shape))
    return pl.pallas_call(
        _ssd_kernel,
        grid=(B,),
        in_specs=[b3(SSD_CONV_CH), b3(SSD_WIDTH), b3(SSD_WIDTH),
                  pl.BlockSpec((None, nc, SSD_WIDTH), lambda b: (b, 0, 0)),
                  const((SSD_CONV, SSD_CONV_CH)), const((1, SSD_CONV_CH)),
                  const((1, SSD_WIDTH)), const((1, SSD_WIDTH)), const((1, SSD_WIDTH))],
        out_specs=b3(SSD_WIDTH),
        out_shape=jax.ShapeDtypeStruct((B, S, SSD_WIDTH), BF16),
        scratch_shapes=[pltpu.VMEM((S, SSD_CONV_CH), F32), pltpu.VMEM((S, SSD_WIDTH), F32),
                        pltpu.VMEM((SSD_GROUPS * SSD_STATE, SSD_WIDTH), F32)],
        compiler_params=_cparams("parallel"),
        name="ssd_mixer",
    )(xbc.reshape(B, S, SSD_CONV_CH), z.reshape(B, S, SSD_WIDTH), dtx.reshape(B, S, SSD_WIDTH),
      r, conv_w, conv_b, alogx, dx, norm_g)


def _memkv_kernel(mem_ref, g_ref, w_ref, kv_ref):
    kv_ref[...] = _dot(_rms(mem_ref[...], g_ref[...]).astype(BF16), w_ref[...]).astype(kv_ref.dtype)


def _memkv(mem, g_mem, w_xkv, layer):
    B, M, D = mem.shape
    return pl.pallas_call(
        _memkv_kernel,
        grid=(B,),
        in_specs=[pl.BlockSpec((None, M, D), lambda b: (b, 0, 0)),
                  pl.BlockSpec((1, D), lambda b: (0, 0)),
                  pl.BlockSpec((None, D, 2 * X_WIDTH), lambda b: (layer, 0, 0))],
        out_specs=pl.BlockSpec((None, M, 2 * X_WIDTH), lambda b: (b, 0, 0)),
        out_shape=jax.ShapeDtypeStruct((B, M, 2 * X_WIDTH), BF16),
        compiler_params=_cparams("parallel"),
        name="mem_kv",
    )(mem, g_mem, w_xkv)


def _route(lg, le):
    lane = lax.broadcasted_iota(jnp.int32, lg.shape, 1)
    lane_f = lane.astype(F32)
    big = float(LANES)
    lgm = jnp.where(lane < N_EGROUPS, lg, NEG_BIG)
    gmax = jnp.max(lgm, axis=-1, keepdims=True)
    g_p = 1.0 / jnp.sum(jnp.exp(lgm - gmax), axis=-1, keepdims=True)
    g_idx = jnp.min(jnp.where(lgm == gmax, lane_f, big), axis=-1, keepdims=True)
    in_grp = (lane // EXPERTS_PER_GROUP == g_idx.astype(jnp.int32)) & (lane < N_EXPERTS)
    e1 = jnp.where(in_grp, le, NEG_BIG)
    v1 = jnp.max(e1, axis=-1, keepdims=True)
    i1 = jnp.min(jnp.where(e1 == v1, lane_f, big), axis=-1, keepdims=True)
    e2 = jnp.where(lane_f == i1, NEG_BIG, e1)
    v2 = jnp.max(e2, axis=-1, keepdims=True)
    i2 = jnp.min(jnp.where(e2 == v2, lane_f, big), axis=-1, keepdims=True)
    t = jnp.exp(v2 - v1)
    w1 = g_p / (1.0 + t)
    w2 = g_p * t / (1.0 + t)
    return lane, i1.astype(jnp.int32), i2.astype(jnp.int32), w1, w2


def _merge_kernel(x_ref, oa_ref, ob_ref, oc_ref, gate_ref, wb_ref, wmix_ref, gx_ref, wq_ref,
                  kv_ref, wo_ref, gm_ref, wr_ref, br_ref, x2_ref, hp_ref, route_ref, cnt_ref):
    tm, D = x_ref.shape
    k = kv_ref[:, 0:X_WIDTH]
    v = kv_ref[:, X_WIDTH:2 * X_WIDTH]
    vlane = lax.broadcasted_iota(jnp.int32, v.shape, 1) // X_HEAD_DIM

    def part(r0, n):
        rows = slice(r0, r0 + n)
        merged = gate_ref[rows, 0:D].astype(F32) * _dot(oa_ref[rows, :], wb_ref[0])
        merged = merged + gate_ref[rows, D:2 * D].astype(F32) * _dot(ob_ref[rows, :], wb_ref[1])
        merged = merged + gate_ref[rows, 2 * D:3 * D].astype(F32) * _dot(oc_ref[rows, :], wb_ref[2])
        x1 = x_ref[rows, :] + _dot(merged.astype(BF16), wmix_ref[...])

        hq = _rms(x1, gx_ref[...]).astype(BF16)
        q = (_dot(hq, wq_ref[...]) * (X_HEAD_DIM ** -0.5)).astype(BF16)
        qlane = lax.broadcasted_iota(jnp.int32, q.shape, 1) // X_HEAD_DIM
        o = jnp.zeros(q.shape, F32)
        for hd in range(X_HEADS):
            s = _dot_nt(jnp.where(qlane == hd, q, jnp.zeros_like(q)), k)
            p = jnp.exp(s - jnp.max(s, axis=-1, keepdims=True))
            p = p / jnp.sum(p, axis=-1, keepdims=True)
            o = o + _dot(p.astype(BF16), jnp.where(vlane == hd, v, jnp.zeros_like(v)))
        x2 = x1 + _dot(o.astype(BF16), wo_ref[...])
        x2_ref[rows, :] = x2

        hm = _rms(x2, gm_ref[...]).astype(BF16)
        logits = _dot(hm, wr_ref[...]) + br_ref[...]
        lane, i1, i2, w1, w2 = _route(logits[:, 0:LANES], logits[:, LANES:2 * LANES])

        ea = jnp.minimum(i1, i2)
        eb = jnp.maximum(i1, i2)
        wa = jnp.where(i1 < i2, w1, w2)
        wb = jnp.where(i1 < i2, w2, w1)
        la = ea % EXPERTS_PER_GROUP
        lb = eb % EXPERTS_PER_GROUP
        pair = jnp.where(la == 0, lb - 1, jnp.where(la == 1, lb + 1, 5))
        cls = (ea // EXPERTS_PER_GROUP) * N_PAIRS + pair

        bits = lax.bitcast_convert_type(hm.astype(F32), jnp.uint32)
        packed = (bits[:, 0:D // 2] >> 16) | bits[:, D // 2:D]
        base = r0 * ROW_SUB
        for s in range(H_SUB):
            hp_ref[pl.ds(base + s, n, stride=ROW_SUB), :] = packed[:, s * LANES:(s + 1) * LANES]
        wrow = jnp.where(lane == 0, wa, jnp.where(lane == 1, wb, 0.0))
        hp_ref[pl.ds(base + H_SUB, n, stride=ROW_SUB), :] = lax.bitcast_convert_type(
            wrow, jnp.uint32)
        for s in range(H_SUB + 1, ROW_SUB):
            hp_ref[pl.ds(base + s, n, stride=ROW_SUB), :] = jnp.zeros((n, LANES), jnp.uint32)
        return cls

    cls = part(0, tm)

    @pl.when(pl.program_id(0) == 0)
    def _():
        cnt_ref[...] = jnp.zeros(cnt_ref.shape, F32)

    lane = lax.broadcasted_iota(jnp.int32, (tm, LANES), 1)
    oh = (lane == cls).astype(F32)
    tr = lax.broadcasted_iota(jnp.int32, (tm, tm), 0)
    tc = lax.broadcasted_iota(jnp.int32, (tm, tm), 1)
    before = jnp.where(tr > tc, 1.0, 0.0).astype(BF16)
    prior = _dot(before, oh.astype(BF16)) + cnt_ref[0:1, :]
    rank = jnp.sum(prior * oh, axis=-1, keepdims=True)
    cnt_ref[...] = cnt_ref[...] + jnp.sum(oh, axis=0, keepdims=True)
    route = jnp.where(lane == 0, cls.astype(F32), 0.0)
    route_ref[...] = jnp.where(lane == 1, rank, route)


def _merge(x, oa, ob, oc, gates, wb, wmix, gx, wq, kv, wo, gm, wr, br, S, layer):
    T, D = x.shape
    tm = TM_MERGE
    per_b = S // tm
    row = lambda n: pl.BlockSpec((tm, n), lambda i: (i, 0))
    const = lambda shape: pl.BlockSpec(shape, lambda i: (0,) * len(shape),
                                       pipeline_mode=pl.Buffered(1))
    stack = lambda shape: pl.BlockSpec((None,) + shape, lambda i: (layer,) + (0,) * len(shape),
                                       pipeline_mode=pl.Buffered(1))
    return pl.pallas_call(
        _merge_kernel,
        grid=(T // tm,),
        in_specs=[row(D), row(512), row(512), row(512), row(3 * D),
                  stack((N_BRANCH, 512, D)), stack((D, D)), const((1, D)), stack((D, X_WIDTH)),
                  pl.BlockSpec((None, N_MEM, 2 * X_WIDTH), lambda i: (i // per_b, 0, 0)),
                  stack((X_WIDTH, D)), const((1, D)), stack((D, 2 * LANES)),
                  const((1, 2 * LANES))],
        out_specs=(row(D), pl.BlockSpec((tm * ROW_SUB, LANES), lambda i: (i, 0)), row(LANES),
                   pl.BlockSpec((8, LANES), lambda i: (0, 0))),
        out_shape=(jax.ShapeDtypeStruct((T, D), F32),
                   jax.ShapeDtypeStruct((T * ROW_SUB, LANES), jnp.uint32),
                   jax.ShapeDtypeStruct((T, LANES), F32), jax.ShapeDtypeStruct((8, LANES), F32)),
        compiler_params=_cparams("arbitrary"),
        name="merge_xattn_router",
    )(x, oa, ob, oc, gates, wb, wmix, gx, wq, kv, wo, gm, wr, br)


def _plan(route, cnt, T):
    cls = route[:, 0].astype(jnp.int32)
    rank = route[:, 1].astype(jnp.int32)
    counts = cnt[0, :N_CLASSES].astype(jnp.int32)
    padded = ((counts + TME - 1) // TME) * TME
    ends = jnp.cumsum(padded)
    off = ends - padded
    onehot = cls[:, None] == jnp.arange(N_CLASSES, dtype=jnp.int32)
    slots = jnp.sum(jnp.where(onehot, off, 0), axis=-1) + rank
    n_tiles = T // TME + N_CLASSES
    tile_start = jnp.arange(n_tiles, dtype=jnp.int32) * TME
    tile_cls = jnp.minimum(jnp.sum(tile_start[:, None] >= ends[None, :], axis=1), N_CLASSES - 1)
    pair = tile_cls % N_PAIRS
    group = tile_cls // N_PAIRS
    la = (pair >= 3).astype(jnp.int32) + (pair >= 5).astype(jnp.int32)
    lb = jnp.where(pair < 3, pair + 1, jnp.where(pair < 5, pair - 1, 3))
    tile_ea = (group * EXPERTS_PER_GROUP + la).astype(jnp.int32)
    tile_eb = (group * EXPERTS_PER_GROUP + lb).astype(jnp.int32)
    n_used = ends[-1] // TME
    tile_src = jnp.where(jnp.arange(n_tiles) < n_used, jnp.arange(n_tiles), 0).astype(jnp.int32)
    pad_tile = jnp.where(padded > 0, ends - TME, -1).astype(jnp.int32)
    return (slots.astype(jnp.int32), tile_ea, tile_eb, tile_src,
            n_used.reshape(1).astype(jnp.int32), pad_tile)


def _row_copy(src, dst, i, j, sem):
    return pltpu.make_async_copy(_row_tile(src, i), _row_tile(dst, j), sem)


def _scatter_kernel(slots_ref, pad_ref, nu_ref, hp_ref, hs_ref, zbuf_ref, sem, zsem):
    tm = hp_ref.shape[0] // ROW_SUB

    def zero_tile(row):
        start = pl.multiple_of(row * ROW_SUB, TME * ROW_SUB)
        return pltpu.make_async_copy(zbuf_ref, hs_ref.at[pl.ds(start, TME * ROW_SUB)], zsem)

    @pl.when(pl.program_id(0) == 0)
    def _():
        zbuf_ref[...] = jnp.zeros(zbuf_ref.shape, zbuf_ref.dtype)
        for e in range(N_CLASSES):
            @pl.when(pad_ref[e] >= 0)
            def _():
                zero_tile(pad_ref[e]).start()
        for e in range(N_CLASSES):
            @pl.when(pad_ref[e] >= 0)
            def _():
                zero_tile(pad_ref[e]).wait()

        def tail(t, c):
            zero_tile(t * TME).start()
            zero_tile(t * TME).wait()
            return c

        lax.fori_loop(nu_ref[0], hs_ref.shape[0] // (TME * ROW_SUB), tail, 0)

    def start(r2, c):
        for k in range(2):
            _row_copy(hp_ref, hs_ref, 2 * r2 + k, slots_ref[2 * r2 + k], sem).start(priority=k)
        return c

    def wait(r2, c):
        for k in range(2):
            _row_copy(hp_ref, hs_ref, 2 * r2 + k, slots_ref[2 * r2 + k], sem).wait()
        return c

    lax.fori_loop(0, tm // 2, start, 0, unroll=8)
    lax.fori_loop(0, tm // 2, wait, 0, unroll=8)


def _scatter(slots, pad_tile, n_used, hp, n_rows):
    T = hp.shape[0] // ROW_SUB
    tm = TM_SCATTER
    return pl.pallas_call(
        _scatter_kernel,
        grid=(T // tm,),
        in_specs=[pl.BlockSpec((tm,), lambda i: (i,), memory_space=pltpu.SMEM),
                  pl.BlockSpec(memory_space=pltpu.SMEM),
                  pl.BlockSpec(memory_space=pltpu.SMEM),
                  pl.BlockSpec((tm * ROW_SUB, LANES), lambda i: (i, 0))],
        out_specs=pl.BlockSpec(memory_space=pl.ANY),
        out_shape=jax.ShapeDtypeStruct((n_rows * ROW_SUB, LANES), hp.dtype),
        scratch_shapes=[pltpu.VMEM((TME * ROW_SUB, LANES), hp.dtype), pltpu.SemaphoreType.DMA(()),
                        pltpu.SemaphoreType.DMA(())],
        compiler_params=_cparams("arbitrary"),
        name="moe_scatter",
    )(slots, pad_tile, n_used, hp)


def _expert_kernel(ea_ref, eb_ref, ts_ref, nu_ref, hs_ref, wga_ref, wua_ref, wda_ref,
                   wgb_ref, wub_ref, wdb_ref, ys_ref, *cache):
    i = pl.program_id(0)
    active = i < nu_ref[0]
    prev = jnp.maximum(i - 1, 0)

    for e_ref, srcs, dsts in ((ea_ref, (wga_ref, wua_ref, wda_ref), cache[0:3]),
                              (eb_ref, (wgb_ref, wub_ref, wdb_ref), cache[3:6])):
        @pl.when(active & ((i == 0) | (e_ref[i] != e_ref[prev])))
        def _():
            for src, dst in zip(srcs, dsts):
                dst[...] = src[...].astype(BF16)

    @pl.when(active)
    def _():
        words = [hs_ref[pl.ds(s, TME, stride=ROW_SUB), :] for s in range(H_SUB)]
        low = [lax.bitcast_convert_type(u << 16, F32) for u in words]
        high = [lax.bitcast_convert_type(u & jnp.uint32(0xFFFF0000), F32) for u in words]
        h = jnp.concatenate(low + high, axis=1).astype(BF16)
        wts = lax.bitcast_convert_type(hs_ref[pl.ds(H_SUB, TME, stride=ROW_SUB), :], F32)
        y = None
        for k in range(2):
            wg, wu, wd = cache[3 * k:3 * k + 3]
            a = _silu(_dot(h, wg[...])) * _dot(h, wu[...])
            yk = wts[:, k:k + 1] * _dot(a.astype(BF16), wd[...])
            y = yk if y is None else y + yk
        _store_rows(ys_ref, y)

    @pl.when(i >= nu_ref[0])
    def _():
        ys_ref[...] = jnp.zeros(ys_ref.shape, ys_ref.dtype)


def _experts(tile_ea, tile_eb, tile_src, n_used, hs, wg, wu, wd, layer):
    P = hs.shape[0] // ROW_SUB
    D = ROW_SUB * LANES

    def wspec(which, a, b):
        return pl.BlockSpec((None, None, a, b),
                            lambda i, ea, eb, ts, nu: (layer, (ea, eb)[which][i], 0, 0))

    shapes = ((D, EXPERT_FF), (D, EXPERT_FF), (EXPERT_FF, D))
    grid_spec = pltpu.PrefetchScalarGridSpec(
        num_scalar_prefetch=4,
        grid=(P // TME,),
        in_specs=([pl.BlockSpec((TME * ROW_SUB, LANES), lambda i, ea, eb, ts, nu: (ts[i], 0))]
                  + [wspec(0, *s) for s in shapes] + [wspec(1, *s) for s in shapes]),
        out_specs=pl.BlockSpec((TME * ROW_SUB, LANES), lambda i, ea, eb, ts, nu: (i, 0)),
        scratch_shapes=[pltpu.VMEM(s, BF16) for s in shapes + shapes])
    return pl.pallas_call(
        _expert_kernel,
        grid_spec=grid_spec,
        out_shape=jax.ShapeDtypeStruct((P * ROW_SUB, LANES), F32),
        compiler_params=_cparams("arbitrary"),
        name="moe_experts",
    )(tile_ea, tile_eb, tile_src, n_used, hs, wg, wu, wd, wg, wu, wd)


def _final_kernel(slots_ref, next_slots_ref, x2_ref, ys_ref, gf_ref, o_ref, ybuf_ref, sem):
    y = _combined_residual(slots_ref, next_slots_ref, x2_ref, ys_ref, ybuf_ref, sem,
                           x2_ref.shape[0])
    o_ref[...] = _rms(y, gf_ref[...])


def _final(slots, x2, ys, g_final):
    T, D = x2.shape
    tm = TM_MOE
    n_steps = T // tm
    row = lambda n: pl.BlockSpec((tm, n), lambda i: (i, 0))
    return pl.pallas_call(
        _final_kernel,
        grid=(n_steps,),
        in_specs=_slot_specs(tm, n_steps) + [row(D), pl.BlockSpec(memory_space=pl.ANY),
                                             pl.BlockSpec((1, D), lambda i: (0, 0))],
        out_specs=row(D),
        out_shape=jax.ShapeDtypeStruct((T, D), F32),
        scratch_shapes=_row_buffers(tm),
        compiler_params=_cparams("arbitrary"),
        name="moe_combine_final",
    )(slots, slots, x2, ys, g_final)


def _prep_inproj_weight(w_in):
    col_dt = 2560 + SSD_CONV_CH
    w_dt = w_in[:, :, col_dt:col_dt + SSD_HEADS]
    return jnp.concatenate(
        [w_in[:, :, :col_dt], w_in[:, :, col_dt + SSD_HEADS:],
         jnp.repeat(w_dt, SSD_HEAD_DIM, axis=2),
         jnp.pad(w_dt, ((0, 0), (0, 0), (0, LANES - SSD_HEADS)))], axis=2).astype(BF16)


def _pad_lanes(v, n=LANES):
    return jnp.pad(v, (0, n - v.shape[0]))[None, :]


def kernel(x, mem, w_in, b_gate, lam_q1, lam_k1, lam_q2, lam_k2, attn_subln_g, rel_bias_table,
           w_pool_group, pool_scale, ssd_conv_w, ssd_conv_b, ssd_dt_bias, ssd_A_log, ssd_D,
           ssd_norm_g, w_branch, w_mix_out, g_mix, g_xattn, g_mem, w_xq, w_xkv, w_xo, g_moe,
           w_router_group, b_router_group, w_router_expert, b_router_expert,
           w_exp_gate, w_exp_up, w_exp_down, g_final):
    B, S, D = x.shape
    T = B * S
    depth = w_in.shape[0]
    nc = S // CHUNK
    lam_inits = tuple(0.8 - 0.6 * math.exp(-0.3 * l) for l in range(depth))
    bias, lam = _setup(rel_bias_table, lam_q1, lam_k1, lam_q2, lam_k2, lam_inits)

    xf = x.reshape(T, D)
    w_proj = _prep_inproj_weight(w_in)
    w_branch_b, w_mix_b = w_branch.astype(BF16), w_mix_out.astype(BF16)
    w_xq_b, w_xkv_b, w_xo_b = w_xq.astype(BF16), w_xkv.astype(BF16), w_xo.astype(BF16)
    w_r = jnp.concatenate(
        [jnp.pad(w_router_group, ((0, 0), (0, 0), (0, LANES - N_EGROUPS))),
         jnp.pad(w_router_expert, ((0, 0), (0, 0), (0, LANES - N_EXPERTS)))], axis=2).astype(BF16)
    moe = None
    for l in range(depth):
        outs = _inproj(
            xf, g_mix[l][None, :], w_proj, b_gate[l][None, :],
            jnp.repeat(ssd_dt_bias[l], SSD_HEAD_DIM)[None, :], _pad_lanes(ssd_dt_bias[l]),
            _pad_lanes(ssd_A_log[l]), l, moe)
        if moe is not None:
            xf, outs = outs[0], outs[1:]
        qkv, pool_u, z, xbc, gates, dtx, cs8 = outs
        o_a = _attention(qkv, bias, lam[l][None, :], attn_subln_g[l][None, :],
                         1.0 - lam_inits[l], B, S)
        o_b = _pool(pool_u, w_pool_group[l].astype(BF16), pool_scale[l][None, :], B, S)
        r = cs8[:, :SSD_HEADS].reshape(B, nc, CHUNK, SSD_HEADS).transpose(0, 1, 3, 2)
        r = r.reshape(B, nc, SSD_WIDTH)
        o_c = _ssd(xbc, z, dtx, r, ssd_conv_w[l], ssd_conv_b[l][None, :],
                   jnp.repeat(ssd_A_log[l], SSD_HEAD_DIM)[None, :],
                   jnp.repeat(ssd_D[l], SSD_HEAD_DIM)[None, :], ssd_norm_g[l][None, :], B, S)
        kv = _memkv(mem, g_mem[l][None, :], w_xkv_b, l)
        b_r = jnp.concatenate([_pad_lanes(b_router_group[l]), _pad_lanes(b_router_expert[l])],
                              axis=1)
        x2, hp, route, cnt = _merge(
            xf, o_a.reshape(T, A_WIDTH), o_b.reshape(T, POOL_WIDTH), o_c.reshape(T, SSD_WIDTH),
            gates, w_branch_b, w_mix_b, g_xattn[l][None, :], w_xq_b, kv, w_xo_b,
            g_moe[l][None, :], w_r, b_r, S, l)
        slots, tile_ea, tile_eb, tile_src, n_used, pad_tile = _plan(route, cnt, T)
        hs = _scatter(slots, pad_tile, n_used, hp, T + N_CLASSES * TME)
        ys = _experts(tile_ea, tile_eb, tile_src, n_used, hs, w_exp_gate, w_exp_up, w_exp_down, l)
        xf, moe = x2, (slots, ys)
    return _final(moe[0], xf, moe[1], g_final[None, :]).reshape(B, S, D)
```

```python
import functools
import math

import jax
import jax.numpy as jnp
from jax import lax
from jax.experimental import pallas as pl
from jax.experimental.pallas import tpu as pltpu

F32 = jnp.float32
BF16 = jnp.bfloat16

EPS = 1e-6
CHUNK = 64
N_MEM = 256

A_HEADS = 4
A_QK_DIM = 64
A_V_DIM = 128
A_WIDTH = 512
REL_BUCKETS = 32
REL_MAX_DIST = 128

POOL_WINDOWS = (2, 4, 8, 16)
POOL_GROUP_DIM = 128
POOL_WIDTH = 512

SSD_HEADS = 8
SSD_HEAD_DIM = 64
SSD_WIDTH = 512
SSD_GROUPS = 2
SSD_STATE = 64
SSD_CONV = 4
SSD_CONV_CH = 768

N_BRANCH = 3
X_HEADS = 4
X_HEAD_DIM = 64
X_WIDTH = 256

N_EGROUPS = 4
EXPERTS_PER_GROUP = 4
N_EXPERTS = 16
EXPERT_FF = 512

LANES = 128
NEG_BIG = -1e30

C_Q, C_K, C_V, C_POOL, C_Z, C_XBC, C_GATE = 0, 512, 1024, 1536, 2048, 2560, 3328
C_DTX = 6400
C_DT8 = 6912
C_END = 7040

TM_PROJ = 512
TM_MERGE = 512
TM_MOE = 512
TM_SCATTER = 1024
SLOT_BLOCK = 1024
TME = 512
N_PAIRS = 6
N_CLASSES = N_EGROUPS * N_PAIRS
H_SUB = 4
TA = 256
VMEM_LIMIT = 56 * 1024 * 1024


def _cparams(*sem):
    return pltpu.CompilerParams(dimension_semantics=sem, vmem_limit_bytes=VMEM_LIMIT)


def _rms(x, g):
    return x * lax.rsqrt(jnp.mean(x * x, axis=-1, keepdims=True) + EPS) * g


def _dot(a, b):
    return jnp.dot(a, b, preferred_element_type=F32)


def _dot_nt(a, b):
    return lax.dot_general(a, b, (((1,), (1,)), ((), ())), preferred_element_type=F32)


def _dot_tn(a, b):
    return lax.dot_general(a, b, (((0,), (0,)), ((), ())), preferred_element_type=F32)


def _sigmoid(x):
    return 1.0 / (1.0 + jnp.exp(-x))


def _silu(x):
    return x * _sigmoid(x)


def _softplus(x):
    return jnp.maximum(x, 0.0) + jnp.log(1.0 + jnp.exp(-jnp.abs(x)))


ROW_SUB = 8


def _store_rows(ref, x):
    m = x.shape[0]
    for s in range(ROW_SUB):
        ref[pl.ds(s, m, stride=ROW_SUB), :] = x[:, s * LANES:(s + 1) * LANES]


def _load_rows(ref):
    m = ref.shape[0] // ROW_SUB
    return jnp.concatenate([ref[pl.ds(s, m, stride=ROW_SUB), :] for s in range(ROW_SUB)], axis=1)


def _row_tile(ref, r):
    return ref.at[pl.ds(pl.multiple_of(r * ROW_SUB, ROW_SUB), ROW_SUB)]


def _shift_rows(v, s, row):
    return jnp.where(row >= s, pltpu.roll(v, s, axis=0), 0.0)


def _chunk_cumsum(v, row_in_chunk):
    s = 1
    while s < CHUNK:
        v = v + jnp.where(row_in_chunk >= s, pltpu.roll(v, s, axis=0), 0.0)
        s *= 2
    return v


def _setup_kernel(table_ref, bucket_ref, vis_ref, lq1_ref, lk1_ref, lq2_ref, lk2_ref,
                  bias_ref, lam_ref, *, lam_inits):
    for h in range(A_HEADS):
        tiles = []
        for d in range(3):
            bkt = bucket_ref[d]
            acc = jnp.zeros(bkt.shape, F32)
            for b in range(REL_BUCKETS):
                acc = acc + jnp.where(bkt == b, table_ref[b, h], 0.0)
            tiles.append(acc)
        bias_ref[h, 0] = jnp.where(vis_ref[...] > 0, tiles[0] - tiles[2], NEG_BIG)
        bias_ref[h, 1] = tiles[1] - tiles[2]
    s1 = jnp.sum(lq1_ref[...] * lk1_ref[...], axis=-1, keepdims=True)
    s2 = jnp.sum(lq2_ref[...] * lk2_ref[...], axis=-1, keepdims=True)
    row = lax.broadcasted_iota(jnp.int32, s1.shape, 0)
    init = jnp.zeros(s1.shape, F32)
    for l, v in enumerate(lam_inits):
        init = jnp.where(row == l, v, init)
    lam_ref[...] = jnp.broadcast_to(jnp.exp(s1) - jnp.exp(s2) + init, lam_ref.shape)


def _rel_bucket_index(rel):
    nb = REL_BUCKETS // 2
    max_exact = nb // 2
    side = jnp.where(rel > 0, nb, 0)
    n = jnp.abs(rel)
    n_f = jnp.maximum(n, 1).astype(F32)
    large = max_exact + (jnp.log(n_f / max_exact) / math.log(REL_MAX_DIST / max_exact)
                         * (nb - max_exact)).astype(jnp.int32)
    large = jnp.minimum(large, nb - 1)
    return side + jnp.where(n < max_exact, n, large)


def _setup(rel_bias_table, lam_q1, lam_k1, lam_q2, lam_k2, lam_inits):
    depth = lam_q1.shape[0]
    r = jnp.arange(TA)
    rel0 = r[None, :] - r[:, None]
    bucket = jnp.stack([_rel_bucket_index(rel0 - d * TA) for d in range(3)]).astype(jnp.int32)
    vis = ((r[None, :] // CHUNK) <= (r[:, None] // CHUNK)).astype(jnp.int32)
    vm = pl.BlockSpec(memory_space=pltpu.VMEM)
    return pl.pallas_call(
        functools.partial(_setup_kernel, lam_inits=lam_inits),
        out_shape=(jax.ShapeDtypeStruct((A_HEADS, 2, TA, TA), F32),
                   jax.ShapeDtypeStruct((depth, LANES), F32)),
        in_specs=[pl.BlockSpec(memory_space=pltpu.SMEM), vm, vm, vm, vm, vm, vm],
        out_specs=(vm, vm),
        name="setup_bias_lam",
    )(rel_bias_table, bucket, vis, lam_q1, lam_k1, lam_q2, lam_k2)


def _gather_rows(slots_ref, base, ys_ref, ybuf_ref, sem, tm, wait):
    def body(r2, c):
        for k in range(2):
            r = 2 * r2 + k
            cp = pltpu.make_async_copy(_row_tile(ys_ref, slots_ref[base + r]),
                                       _row_tile(ybuf_ref, r), sem)
            if wait:
                cp.wait()
            else:
                cp.start(priority=k)
        return c

    lax.fori_loop(0, tm // 2, body, 0, unroll=8)


def _combined_residual(slots_ref, next_slots_ref, x2_ref, ys_ref, ybuf_ref, sem, tm):
    i = pl.program_id(0)
    cur = i % 2
    per = SLOT_BLOCK // tm
    base = (i % per) * tm
    next_base = ((i + 1) % per) * tm

    @pl.when(i == 0)
    def _():
        _gather_rows(slots_ref, base, ys_ref, ybuf_ref.at[0], sem.at[0], tm, wait=False)

    @pl.when(i + 1 < pl.num_programs(0))
    def _():
        _gather_rows(next_slots_ref, next_base, ys_ref, ybuf_ref.at[1 - cur], sem.at[1 - cur], tm,
                     wait=False)

    _gather_rows(slots_ref, base, ys_ref, ybuf_ref.at[cur], sem.at[cur], tm, wait=True)
    return x2_ref[...] + _load_rows(ybuf_ref.at[cur])


def _inproj_kernel(*refs, combine):
    if combine:
        (slots_ref, next_slots_ref, x2_ref, ys_ref, g_ref, w_ref, bg_ref, dtbx_ref,
         dtb8_ref, alog8_ref, x_ref, qkv_ref, pool_ref, z_ref, xbc_ref, gate_ref, dtx_ref,
         cs8_ref, ybuf_ref, sem) = refs
        x = _combined_residual(slots_ref, next_slots_ref, x2_ref, ys_ref, ybuf_ref, sem,
                               x2_ref.shape[0])
        x_ref[...] = x
    else:
        (x_ref, g_ref, w_ref, bg_ref, dtbx_ref, dtb8_ref, alog8_ref,
         qkv_ref, pool_ref, z_ref, xbc_ref, gate_ref, dtx_ref, cs8_ref) = refs
        x = x_ref[...]
    h = _rms(x, g_ref[...]).astype(BF16)

    def mm(lo, hi):
        return _dot(h, w_ref[:, lo:hi])

    qkv_ref[:, 0:512] = (mm(C_Q, C_K) * (A_QK_DIM ** -0.5)).astype(BF16)
    qkv_ref[:, 512:1024] = mm(C_K, C_V).astype(BF16)
    qkv_ref[:, 1024:1536] = mm(C_V, C_POOL).astype(BF16)
    pool_ref[...] = mm(C_POOL, C_Z).astype(BF16)
    z_ref[...] = mm(C_Z, C_XBC).astype(BF16)
    xbc_ref[:, 0:512] = mm(C_XBC, C_XBC + 512).astype(BF16)
    xbc_ref[:, 512:768] = mm(C_XBC + 512, C_GATE).astype(BF16)
    for j in range(6):
        lo = C_GATE + 512 * j
        gate_ref[:, 512 * j:512 * (j + 1)] = _sigmoid(
            mm(lo, lo + 512) + bg_ref[:, 512 * j:512 * (j + 1)]).astype(BF16)
    dtx_ref[...] = _softplus(mm(C_DTX, C_DT8) + dtbx_ref[...])
    dt8 = _softplus(mm(C_DT8, C_END) + dtb8_ref[...])
    row = lax.broadcasted_iota(jnp.int32, dt8.shape, 0) % CHUNK
    cs8_ref[...] = _chunk_cumsum(dt8 * (-jnp.exp(alog8_ref[...])), row)


def _slot_specs(tm, n_steps):
    per = SLOT_BLOCK // tm
    return [pl.BlockSpec((SLOT_BLOCK,), lambda i: (i // per,), memory_space=pltpu.SMEM),
            pl.BlockSpec((SLOT_BLOCK,), lambda i: (jnp.minimum(i + 1, n_steps - 1) // per,),
                         memory_space=pltpu.SMEM)]


def _row_buffers(tm):
    return [pltpu.VMEM((2, tm * ROW_SUB, LANES), F32), pltpu.SemaphoreType.DMA((2,))]


def _inproj(x, g, w, bg, dtbx, dtb8, alog8, layer, moe=None):
    T, D = x.shape
    tm = TM_PROJ
    n_steps = T // tm
    const = lambda shape: pl.BlockSpec(shape, lambda i: (0, 0), pipeline_mode=pl.Buffered(1))
    row = lambda n: pl.BlockSpec((tm, n), lambda i: (i, 0))
    in_specs = [row(D), const((1, D)),
                pl.BlockSpec((None, D, C_END), lambda i: (layer, 0, 0),
                             pipeline_mode=pl.Buffered(1)),
                const((1, 3072)), const((1, 512)), const((1, LANES)), const((1, LANES))]
    out_specs = [row(1536), row(512), row(512), row(768), row(3072), row(512), row(LANES)]
    out_shape = [jax.ShapeDtypeStruct((T, 1536), BF16), jax.ShapeDtypeStruct((T, 512), BF16),
                 jax.ShapeDtypeStruct((T, 512), BF16), jax.ShapeDtypeStruct((T, 768), BF16),
                 jax.ShapeDtypeStruct((T, 3072), BF16), jax.ShapeDtypeStruct((T, 512), F32),
                 jax.ShapeDtypeStruct((T, LANES), F32)]
    args = (x, g, w, bg, dtbx, dtb8, alog8)
    scratch = []
    if moe is not None:
        slots, ys = moe
        in_specs = (_slot_specs(tm, n_steps) + [row(D), pl.BlockSpec(memory_space=pl.ANY)]
                    + in_specs[1:])
        out_specs = [row(D)] + out_specs
        out_shape = [jax.ShapeDtypeStruct((T, D), F32)] + out_shape
        args = (slots, slots, x, ys) + args[1:]
        scratch = _row_buffers(tm)
    return pl.pallas_call(
        functools.partial(_inproj_kernel, combine=moe is not None),
        grid=(n_steps,),
        in_specs=in_specs,
        out_specs=tuple(out_specs),
        out_shape=tuple(out_shape),
        scratch_shapes=scratch,
        compiler_params=_cparams("arbitrary" if moe is not None else "parallel"),
        name="inproj",
    )(*args)


def _attn_kernel(q_ref, k_ref, v_ref, bias_ref, lam_ref, g_ref, o_ref, vext_ref, *, out_scale):
    S = q_ref.shape[0]
    lam = lam_ref[...]
    g = g_ref[...]
    vext_ref[:, 0:A_V_DIM] = v_ref[...]
    vext_ref[:, A_V_DIM:2 * A_V_DIM] = jnp.ones((S, A_V_DIM), vext_ref.dtype)
    for i in range(S // TA):
        q = q_ref[i * TA:(i + 1) * TA, :]
        lane = lax.broadcasted_iota(jnp.int32, q.shape, 1)
        qq = jnp.concatenate([jnp.where(lane < A_QK_DIM, q, jnp.zeros_like(q)),
                              jnp.where(lane >= A_QK_DIM, q, jnp.zeros_like(q))], axis=0)
        pieces = []
        if i >= 2:
            pieces.append((0, (i - 1) * TA, None))
        if i >= 1:
            pieces.append(((i - 1) * TA, i * TA, 1))
        pieces.append((i * TA, (i + 1) * TA, 0))
        ss = []
        for lo, hi, bi in pieces:
            s = _dot_nt(qq, k_ref[lo:hi, :])
            if bi is not None:
                s = s + jnp.concatenate([bias_ref[bi], bias_ref[bi]], axis=0)
            ss.append(s)
        m = jnp.max(ss[0], axis=-1, keepdims=True)
        for s in ss[1:]:
            m = jnp.maximum(m, jnp.max(s, axis=-1, keepdims=True))
        acc = jnp.zeros((2 * TA, 2 * A_V_DIM), F32)
        for s, (lo, hi, _) in zip(ss, pieces):
            acc = acc + _dot(jnp.exp(s - m).astype(BF16), vext_ref[lo:hi, :])
        out = acc[:, 0:A_V_DIM] / acc[:, A_V_DIM:A_V_DIM + 1]
        o = out[0:TA] - lam * out[TA:2 * TA]
        o_ref[i * TA:(i + 1) * TA, :] = (_rms(o, g) * out_scale).astype(o_ref.dtype)


def _attention(qkv, bias, lam_row, subln_g, out_scale, B, S):
    qkv3 = qkv.reshape(B, S, 3 * A_WIDTH)
    seq = lambda off: pl.BlockSpec((None, S, LANES), lambda b, h: (b, 0, off + h))
    return pl.pallas_call(
        functools.partial(_attn_kernel, out_scale=out_scale),
        grid=(B, A_HEADS),
        in_specs=[seq(0), seq(A_HEADS), seq(2 * A_HEADS),
                  pl.BlockSpec((None, 2, TA, TA), lambda b, h: (h, 0, 0, 0)),
                  pl.BlockSpec((1, LANES), lambda b, h: (0, 0)),
                  pl.BlockSpec((1, LANES), lambda b, h: (0, 0))],
        out_specs=seq(0),
        out_shape=jax.ShapeDtypeStruct((B, S, A_WIDTH), BF16),
        scratch_shapes=[pltpu.VMEM((S, 2 * A_V_DIM), BF16)],
        compiler_params=_cparams("parallel", "parallel"),
        name="diff_attention",
    )(qkv3, qkv3, qkv3, bias, lam_row, subln_g)


def _pool_kernel(u_ref, w_ref, scale_ref, o_ref):
    S = u_ref.shape[0]
    row = lax.broadcasted_iota(jnp.int32, (S, POOL_GROUP_DIM), 0)
    t = (row + 1).astype(F32)
    for gi, win in enumerate(POOL_WINDOWS):
        sl = slice(gi * POOL_GROUP_DIM, (gi + 1) * POOL_GROUP_DIM)
        u = u_ref[:, sl].astype(F32)
        acc = u
        s = 1
        while s < win:
            acc = acc + _shift_rows(acc, s, row)
            s *= 2
        d = acc / jnp.minimum(t, float(win)) - u
        y = _dot(d.astype(BF16), w_ref[gi])
        o_ref[:, sl] = (y * scale_ref[:, sl]).astype(o_ref.dtype)


def _pool(u, w_group, scale, B, S):
    u3 = u.reshape(B, S, POOL_WIDTH)
    return pl.pallas_call(
        _pool_kernel,
        grid=(B,),
        in_specs=[pl.BlockSpec((None, S, POOL_WIDTH), lambda b: (b, 0, 0)),
                  pl.BlockSpec((len(POOL_WINDOWS), POOL_GROUP_DIM, POOL_GROUP_DIM),
                               lambda b: (0, 0, 0)),
                  pl.BlockSpec((1, POOL_WIDTH), lambda b: (0, 0))],
        out_specs=pl.BlockSpec((None, S, POOL_WIDTH), lambda b: (b, 0, 0)),
        out_shape=jax.ShapeDtypeStruct((B, S, POOL_WIDTH), BF16),
        compiler_params=_cparams("parallel"),
        name="pool_mixer",
    )(u3, w_group, scale)


def _ssd_kernel(xbc_ref, z_ref, dtx_ref, r_ref, convw_ref, convb_ref, alogx_ref, dx_ref, g_ref,
                o_ref, conv_ref, csx_ref, state_ref):
    S = xbc_ref.shape[0]
    nc = S // CHUNK
    GN = SSD_GROUPS * SSD_STATE
    HP = SSD_WIDTH

    xin = xbc_ref[...].astype(F32)
    row = lax.broadcasted_iota(jnp.int32, xin.shape, 0)
    acc = xin * convw_ref[SSD_CONV - 1:SSD_CONV, :] + convb_ref[...]
    for s in range(1, SSD_CONV):
        acc = acc + _shift_rows(xin, s, row) * convw_ref[SSD_CONV - 1 - s:SSD_CONV - s, :]
    conv_ref[...] = _silu(acc)

    row_c = lax.broadcasted_iota(jnp.int32, (S, HP), 0) % CHUNK
    csx_ref[...] = _chunk_cumsum(dtx_ref[...] * (-jnp.exp(alogx_ref[...])), row_c)

    state_ref[...] = jnp.zeros(state_ref.shape, F32)

    li = lax.broadcasted_iota(jnp.int32, (CHUNK, HP), 0)
    ci = lax.broadcasted_iota(jnp.int32, (CHUNK, HP), 1)
    tril = li >= (ci % CHUNK)
    br = lax.broadcasted_iota(jnp.int32, (HP, GN), 0)
    bc = lax.broadcasted_iota(jnp.int32, (HP, GN), 1)
    bmask = (br // (HP // SSD_GROUPS)) == (bc // SSD_STATE)
    sr = lax.broadcasted_iota(jnp.int32, (GN, HP), 0)
    sc = lax.broadcasted_iota(jnp.int32, (GN, HP), 1)
    smask = (sr // SSD_STATE) == (sc // (HP // SSD_GROUPS))
    GW = HP // SSD_GROUPS
    dr = lax.broadcasted_iota(jnp.int32, (GW, GW), 0)
    dc = lax.broadcasted_iota(jnp.int32, (GW, GW), 1)
    dmask = (dr // CHUNK) == (dc // SSD_HEAD_DIM)

    def chunk_body(c, carry):
        r0 = pl.multiple_of(c * CHUNK, CHUNK)
        rows = pl.ds(r0, CHUNK)
        xs = conv_ref[rows, 0:HP]
        Bc = conv_ref[rows, HP:HP + GN].astype(BF16)
        Cc = conv_ref[rows, HP + GN:HP + 2 * GN].astype(BF16)
        dtx = dtx_ref[rows, :]
        cs = csx_ref[rows, :]
        cs_row = r_ref[pl.ds(c, 1), :]
        cs_end = csx_ref[pl.ds(r0 + CHUNK - 1, 1), :]

        xdt = xs * dtx
        Lm = jnp.exp(jnp.where(tril, cs - cs_row, NEG_BIG))
        b_rep = jnp.where(bmask, jnp.concatenate([Bc] * SSD_HEADS, axis=0), jnp.zeros((), BF16))
        scores = _dot_nt(Cc, b_rep)
        G = (scores * Lm).astype(BF16)

        xdt_b = xdt.astype(BF16)
        y_parts = []
        for g in range(SSD_GROUPS):
            xg = xdt_b[:, g * GW:(g + 1) * GW]
            xbd = jnp.where(dmask, jnp.concatenate([xg] * (GW // CHUNK), axis=0),
                            jnp.zeros((), BF16))
            y_parts.append(_dot(G[:, g * GW:(g + 1) * GW], xbd))
        y = jnp.concatenate(y_parts, axis=1)

        st = state_ref[...]
        y = y + _dot(Cc, st.astype(BF16)) * jnp.exp(cs)
        xdec = (xdt * jnp.exp(cs_end - cs)).astype(BF16)
        upd = jnp.where(smask, _dot_tn(Bc, xdec), 0.0)
        state_ref[...] = st * jnp.exp(cs_end) + upd

        y = y + dx_ref[...] * xs
        y = y * _silu(z_ref[rows, :].astype(F32))
        o_ref[rows, :] = _rms(y, g_ref[...]).astype(o_ref.dtype)
        return carry

    lax.fori_loop(0, nc, chunk_body, 0, unroll=4)


def _ssd(xbc, z, dtx, r, conv_w, conv_b, alogx, dx, norm_g, B, S):
    nc = S // CHUNK
    b3 = lambda n: pl.BlockSpec((None, S, n), lambda b: (b, 0, 0))
    const = lambda shape: pl.BlockSpec(shape, lambda b: (0,) * len(shape))
    return pl.pallas_call(
        _ssd_kernel,
        grid=(B,),
        in_specs=[b3(SSD_CONV_CH), b3(SSD_WIDTH), b3(SSD_WIDTH),
                  pl.BlockSpec((None, nc, SSD_WIDTH), lambda b: (b, 0, 0)),
                  const((SSD_CONV, SSD_CONV_CH)), const((1, SSD_CONV_CH)),
                  const((1, SSD_WIDTH)), const((1, SSD_WIDTH)), const((1, SSD_WIDTH))],
        out_specs=b3(SSD_WIDTH),
        out_shape=jax.ShapeDtypeStruct((B, S, SSD_WIDTH), BF16),
        scratch_shapes=[pltpu.VMEM((S, SSD_CONV_CH), F32), pltpu.VMEM((S, SSD_WIDTH), F32),
                        pltpu.VMEM((SSD_GROUPS * SSD_STATE, SSD_WIDTH), F32)],
        compiler_params=_cparams("parallel"),
        name="ssd_mixer",
    )(xbc.reshape(B, S, SSD_CONV_CH), z.reshape(B, S, SSD_WIDTH), dtx.reshape(B, S, SSD_WIDTH),
      r, conv_w, conv_b, alogx, dx, norm_g)


def _memkv_kernel(mem_ref, g_ref, w_ref, kv_ref):
    kv_ref[...] = _dot(_rms(mem_ref[...], g_ref[...]).astype(BF16), w_ref[...]).astype(kv_ref.dtype)


def _memkv(mem, g_mem, w_xkv, layer):
    B, M, D = mem.shape
    return pl.pallas_call(
        _memkv_kernel,
        grid=(B,),
        in_specs=[pl.BlockSpec((None, M, D), lambda b: (b, 0, 0)),
                  pl.BlockSpec((1, D), lambda b: (0, 0)),
                  pl.BlockSpec((None, D, 2 * X_WIDTH), lambda b: (layer, 0, 0))],
        out_specs=pl.BlockSpec((None, M, 2 * X_WIDTH), lambda b: (b, 0, 0)),
        out_shape=jax.ShapeDtypeStruct((B, M, 2 * X_WIDTH), BF16),
        compiler_params=_cparams("parallel"),
        name="mem_kv",
    )(mem, g_mem, w_xkv)


def _route(lg, le):
    lane = lax.broadcasted_iota(jnp.int32, lg.shape, 1)
    lane_f = lane.astype(F32)
    big = float(LANES)
    lgm = jnp.where(lane < N_EGROUPS, lg, NEG_BIG)
    gmax = jnp.max(lgm, axis=-1, keepdims=True)
    g_p = 1.0 / jnp.sum(jnp.exp(lgm - gmax), axis=-1, keepdims=True)
    g_idx = jnp.min(jnp.where(lgm == gmax, lane_f, big), axis=-1, keepdims=True)
    in_grp = (lane // EXPERTS_PER_GROUP == g_idx.astype(jnp.int32)) & (lane < N_EXPERTS)
    e1 = jnp.where(in_grp, le, NEG_BIG)
    v1 = jnp.max(e1, axis=-1, keepdims=True)
    i1 = jnp.min(jnp.where(e1 == v1, lane_f, big), axis=-1, keepdims=True)
    e2 = jnp.where(lane_f == i1, NEG_BIG, e1)
    v2 = jnp.max(e2, axis=-1, keepdims=True)
    i2 = jnp.min(jnp.where(e2 == v2, lane_f, big), axis=-1, keepdims=True)
    t = jnp.exp(v2 - v1)
    w1 = g_p / (1.0 + t)
    w2 = g_p * t / (1.0 + t)
    return lane, i1.astype(jnp.int32), i2.astype(jnp.int32), w1, w2


def _merge_kernel(x_ref, oa_ref, ob_ref, oc_ref, gate_ref, wb_ref, wmix_ref, gx_ref, wq_ref,
                  kv_ref, wo_ref, gm_ref, wr_ref, br_ref, x2_ref, hp_ref, route_ref, cnt_ref):
    tm, D = x_ref.shape
    k = kv_ref[:, 0:X_WIDTH]
    v = kv_ref[:, X_WIDTH:2 * X_WIDTH]
    vlane = lax.broadcasted_iota(jnp.int32, v.shape, 1) // X_HEAD_DIM

    def part(r0, n):
        rows = slice(r0, r0 + n)
        merged = gate_ref[rows, 0:D].astype(F32) * _dot(oa_ref[rows, :], wb_ref[0])
        merged = merged + gate_ref[rows, D:2 * D].astype(F32) * _dot(ob_ref[rows, :], wb_ref[1])
        merged = merged + gate_ref[rows, 2 * D:3 * D].astype(F32) * _dot(oc_ref[rows, :], wb_ref[2])
        x1 = x_ref[rows, :] + _dot(merged.astype(BF16), wmix_ref[...])

        hq = _rms(x1, gx_ref[...]).astype(BF16)
        q = (_dot(hq, wq_ref[...]) * (X_HEAD_DIM ** -0.5)).astype(BF16)
        qlane = lax.broadcasted_iota(jnp.int32, q.shape, 1) // X_HEAD_DIM
        o = jnp.zeros(q.shape, F32)
        for hd in range(X_HEADS):
            s = _dot_nt(jnp.where(qlane == hd, q, jnp.zeros_like(q)), k)
            p = jnp.exp(s - jnp.max(s, axis=-1, keepdims=True))
            p = p / jnp.sum(p, axis=-1, keepdims=True)
            o = o + _dot(p.astype(BF16), jnp.where(vlane == hd, v, jnp.zeros_like(v)))
        x2 = x1 + _dot(o.astype(BF16), wo_ref[...])
        x2_ref[rows, :] = x2

        hm = _rms(x2, gm_ref[...]).astype(BF16)
        logits = _dot(hm, wr_ref[...]) + br_ref[...]
        lane, i1, i2, w1, w2 = _route(logits[:, 0:LANES], logits[:, LANES:2 * LANES])

        ea = jnp.minimum(i1, i2)
        eb = jnp.maximum(i1, i2)
        wa = jnp.where(i1 < i2, w1, w2)
        wb = jnp.where(i1 < i2, w2, w1)
        la = ea % EXPERTS_PER_GROUP
        lb = eb % EXPERTS_PER_GROUP
        pair = jnp.where(la == 0, lb - 1, jnp.where(la == 1, lb + 1, 5))
        cls = (ea // EXPERTS_PER_GROUP) * N_PAIRS + pair

        bits = lax.bitcast_convert_type(hm.astype(F32), jnp.uint32)
        packed = (bits[:, 0:D // 2] >> 16) | bits[:, D // 2:D]
        base = r0 * ROW_SUB
        for s in range(H_SUB):
            hp_ref[pl.ds(base + s, n, stride=ROW_SUB), :] = packed[:, s * LANES:(s + 1) * LANES]
        wrow = jnp.where(lane == 0, wa, jnp.where(lane == 1, wb, 0.0))
        hp_ref[pl.ds(base + H_SUB, n, stride=ROW_SUB), :] = lax.bitcast_convert_type(
            wrow, jnp.uint32)
        for s in range(H_SUB + 1, ROW_SUB):
            hp_ref[pl.ds(base + s, n, stride=ROW_SUB), :] = jnp.zeros((n, LANES), jnp.uint32)
        return cls

    cls = part(0, tm)

    @pl.when(pl.program_id(0) == 0)
    def _():
        cnt_ref[...] = jnp.zeros(cnt_ref.shape, F32)

    lane = lax.broadcasted_iota(jnp.int32, (tm, LANES), 1)
    oh = (lane == cls).astype(F32)
    tr = lax.broadcasted_iota(jnp.int32, (tm, tm), 0)
    tc = lax.broadcasted_iota(jnp.int32, (tm, tm), 1)
    before = jnp.where(tr > tc, 1.0, 0.0).astype(BF16)
    prior = _dot(before, oh.astype(BF16)) + cnt_ref[0:1, :]
    rank = jnp.sum(prior * oh, axis=-1, keepdims=True)
    cnt_ref[...] = cnt_ref[...] + jnp.sum(oh, axis=0, keepdims=True)
    route = jnp.where(lane == 0, cls.astype(F32), 0.0)
    route_ref[...] = jnp.where(lane == 1, rank, route)


def _merge(x, oa, ob, oc, gates, wb, wmix, gx, wq, kv, wo, gm, wr, br, S, layer):
    T, D = x.shape
    tm = TM_MERGE
    per_b = S // tm
    row = lambda n: pl.BlockSpec((tm, n), lambda i: (i, 0))
    const = lambda shape: pl.BlockSpec(shape, lambda i: (0,) * len(shape),
                                       pipeline_mode=pl.Buffered(1))
    stack = lambda shape: pl.BlockSpec((None,) + shape, lambda i: (layer,) + (0,) * len(shape),
                                       pipeline_mode=pl.Buffered(1))
    return pl.pallas_call(
        _merge_kernel,
        grid=(T // tm,),
        in_specs=[row(D), row(512), row(512), row(512), row(3 * D),
                  stack((N_BRANCH, 512, D)), stack((D, D)), const((1, D)), stack((D, X_WIDTH)),
                  pl.BlockSpec((None, N_MEM, 2 * X_WIDTH), lambda i: (i // per_b, 0, 0)),
                  stack((X_WIDTH, D)), const((1, D)), stack((D, 2 * LANES)),
                  const((1, 2 * LANES))],
        out_specs=(row(D), pl.BlockSpec((tm * ROW_SUB, LANES), lambda i: (i, 0)), row(LANES),
                   pl.BlockSpec((8, LANES), lambda i: (0, 0))),
        out_shape=(jax.ShapeDtypeStruct((T, D), F32),
                   jax.ShapeDtypeStruct((T * ROW_SUB, LANES), jnp.uint32),
                   jax.ShapeDtypeStruct((T, LANES), F32), jax.ShapeDtypeStruct((8, LANES), F32)),
        compiler_params=_cparams("arbitrary"),
        name="merge_xattn_router",
    )(x, oa, ob, oc, gates, wb, wmix, gx, wq, kv, wo, gm, wr, br)


def _plan(route, cnt, T):
    cls = route[:, 0].astype(jnp.int32)
    rank = route[:, 1].astype(jnp.int32)
    counts = cnt[0, :N_CLASSES].astype(jnp.int32)
    padded = ((counts + TME - 1) // TME) * TME
    ends = jnp.cumsum(padded)
    off = ends - padded
    onehot = cls[:, None] == jnp.arange(N_CLASSES, dtype=jnp.int32)
    slots = jnp.sum(jnp.where(onehot, off, 0), axis=-1) + rank
    n_tiles = T // TME + N_CLASSES
    tile_start = jnp.arange(n_tiles, dtype=jnp.int32) * TME
    tile_cls = jnp.minimum(jnp.sum(tile_start[:, None] >= ends[None, :], axis=1), N_CLASSES - 1)
    pair = tile_cls % N_PAIRS
    group = tile_cls // N_PAIRS
    la = (pair >= 3).astype(jnp.int32) + (pair >= 5).astype(jnp.int32)
    lb = jnp.where(pair < 3, pair + 1, jnp.where(pair < 5, pair - 1, 3))
    tile_ea = (group * EXPERTS_PER_GROUP + la).astype(jnp.int32)
    tile_eb = (group * EXPERTS_PER_GROUP + lb).astype(jnp.int32)
    n_used = ends[-1] // TME
    tile_src = jnp.where(jnp.arange(n_tiles) < n_used, jnp.arange(n_tiles), 0).astype(jnp.int32)
    pad_tile = jnp.where(padded > 0, ends - TME, -1).astype(jnp.int32)
    return (slots.astype(jnp.int32), tile_ea, tile_eb, tile_src,
            n_used.reshape(1).astype(jnp.int32), pad_tile)


def _row_copy(src, dst, i, j, sem):
    return pltpu.make_async_copy(_row_tile(src, i), _row_tile(dst, j), sem)


def _scatter_kernel(slots_ref, pad_ref, nu_ref, hp_ref, hs_ref, zbuf_ref, sem, zsem):
    tm = hp_ref.shape[0] // ROW_SUB

    def zero_tile(row):
        start = pl.multiple_of(row * ROW_SUB, TME * ROW_SUB)
        return pltpu.make_async_copy(zbuf_ref, hs_ref.at[pl.ds(start, TME * ROW_SUB)], zsem)

    @pl.when(pl.program_id(0) == 0)
    def _():
        zbuf_ref[...] = jnp.zeros(zbuf_ref.shape, zbuf_ref.dtype)
        for e in range(N_CLASSES):
            @pl.when(pad_ref[e] >= 0)
            def _():
                zero_tile(pad_ref[e]).start()
        for e in range(N_CLASSES):
            @pl.when(pad_ref[e] >= 0)
            def _():
                zero_tile(pad_ref[e]).wait()

        def tail(t, c):
            zero_tile(t * TME).start()
            zero_tile(t * TME).wait()
            return c

        lax.fori_loop(nu_ref[0], hs_ref.shape[0] // (TME * ROW_SUB), tail, 0)

    def start(r2, c):
        for k in range(2):
            _row_copy(hp_ref, hs_ref, 2 * r2 + k, slots_ref[2 * r2 + k], sem).start(priority=k)
        return c

    def wait(r2, c):
        for k in range(2):
            _row_copy(hp_ref, hs_ref, 2 * r2 + k, slots_ref[2 * r2 + k], sem).wait()
        return c

    lax.fori_loop(0, tm // 2, start, 0, unroll=8)
    lax.fori_loop(0, tm // 2, wait, 0, unroll=8)


def _scatter(slots, pad_tile, n_used, hp, n_rows):
    T = hp.shape[0] // ROW_SUB
    tm = TM_SCATTER
    return pl.pallas_call(
        _scatter_kernel,
        grid=(T // tm,),
        in_specs=[pl.BlockSpec((tm,), lambda i: (i,), memory_space=pltpu.SMEM),
                  pl.BlockSpec(memory_space=pltpu.SMEM),
                  pl.BlockSpec(memory_space=pltpu.SMEM),
                  pl.BlockSpec((tm * ROW_SUB, LANES), lambda i: (i, 0))],
        out_specs=pl.BlockSpec(memory_space=pl.ANY),
        out_shape=jax.ShapeDtypeStruct((n_rows * ROW_SUB, LANES), hp.dtype),
        scratch_shapes=[pltpu.VMEM((TME * ROW_SUB, LANES), hp.dtype), pltpu.SemaphoreType.DMA(()),
                        pltpu.SemaphoreType.DMA(())],
        compiler_params=_cparams("arbitrary"),
        name="moe_scatter",
    )(slots, pad_tile, n_used, hp)


def _expert_kernel(ea_ref, eb_ref, ts_ref, nu_ref, hs_ref, wga_ref, wua_ref, wda_ref,
                   wgb_ref, wub_ref, wdb_ref, ys_ref, *cache):
    i = pl.program_id(0)
    active = i < nu_ref[0]
    prev = jnp.maximum(i - 1, 0)

    for e_ref, srcs, dsts in ((ea_ref, (wga_ref, wua_ref, wda_ref), cache[0:3]),
                              (eb_ref, (wgb_ref, wub_ref, wdb_ref), cache[3:6])):
        @pl.when(active & ((i == 0) | (e_ref[i] != e_ref[prev])))
        def _():
            for src, dst in zip(srcs, dsts):
                dst[...] = src[...].astype(BF16)

    @pl.when(active)
    def _():
        words = [hs_ref[pl.ds(s, TME, stride=ROW_SUB), :] for s in range(H_SUB)]
        low = [lax.bitcast_convert_type(u << 16, F32) for u in words]
        high = [lax.bitcast_convert_type(u & jnp.uint32(0xFFFF0000), F32) for u in words]
        h = jnp.concatenate(low + high, axis=1).astype(BF16)
        wts = lax.bitcast_convert_type(hs_ref[pl.ds(H_SUB, TME, stride=ROW_SUB), :], F32)
        y = None
        for k in range(2):
            wg, wu, wd = cache[3 * k:3 * k + 3]
            a = _silu(_dot(h, wg[...])) * _dot(h, wu[...])
            yk = wts[:, k:k + 1] * _dot(a.astype(BF16), wd[...])
            y = yk if y is None else y + yk
        _store_rows(ys_ref, y)

    @pl.when(i >= nu_ref[0])
    def _():
        ys_ref[...] = jnp.zeros(ys_ref.shape, ys_ref.dtype)


def _experts(tile_ea, tile_eb, tile_src, n_used, hs, wg, wu, wd, layer):
    P = hs.shape[0] // ROW_SUB
    D = ROW_SUB * LANES

    def wspec(which, a, b):
        return pl.BlockSpec((None, None, a, b),
                            lambda i, ea, eb, ts, nu: (layer, (ea, eb)[which][i], 0, 0))

    shapes = ((D, EXPERT_FF), (D, EXPERT_FF), (EXPERT_FF, D))
    grid_spec = pltpu.PrefetchScalarGridSpec(
        num_scalar_prefetch=4,
        grid=(P // TME,),
        in_specs=([pl.BlockSpec((TME * ROW_SUB, LANES), lambda i, ea, eb, ts, nu: (ts[i], 0))]
                  + [wspec(0, *s) for s in shapes] + [wspec(1, *s) for s in shapes]),
        out_specs=pl.BlockSpec((TME * ROW_SUB, LANES), lambda i, ea, eb, ts, nu: (i, 0)),
        scratch_shapes=[pltpu.VMEM(s, BF16) for s in shapes + shapes])
    return pl.pallas_call(
        _expert_kernel,
        grid_spec=grid_spec,
        out_shape=jax.ShapeDtypeStruct((P * ROW_SUB, LANES), F32),
        compiler_params=_cparams("arbitrary"),
        name="moe_experts",
    )(tile_ea, tile_eb, tile_src, n_used, hs, wg, wu, wd, wg, wu, wd)


def _final_kernel(slots_ref, next_slots_ref, x2_ref, ys_ref, gf_ref, o_ref, ybuf_ref, sem):
    y = _combined_residual(slots_ref, next_slots_ref, x2_ref, ys_ref, ybuf_ref, sem,
                           x2_ref.shape[0])
    o_ref[...] = _rms(y, gf_ref[...])


def _final(slots, x2, ys, g_final):
    T, D = x2.shape
    tm = TM_MOE
    n_steps = T // tm
    row = lambda n: pl.BlockSpec((tm, n), lambda i: (i, 0))
    return pl.pallas_call(
        _final_kernel,
        grid=(n_steps,),
        in_specs=_slot_specs(tm, n_steps) + [row(D), pl.BlockSpec(memory_space=pl.ANY),
                                             pl.BlockSpec((1, D), lambda i: (0, 0))],
        out_specs=row(D),
        out_shape=jax.ShapeDtypeStruct((T, D), F32),
        scratch_shapes=_row_buffers(tm),
        compiler_params=_cparams("arbitrary"),
        name="moe_combine_final",
    )(slots, slots, x2, ys, g_final)


def _prep_inproj_weight(w_in):
    col_dt = 2560 + SSD_CONV_CH
    w_dt = w_in[:, :, col_dt:col_dt + SSD_HEADS]
    return jnp.concatenate(
        [w_in[:, :, :col_dt], w_in[:, :, col_dt + SSD_HEADS:],
         jnp.repeat(w_dt, SSD_HEAD_DIM, axis=2),
         jnp.pad(w_dt, ((0, 0), (0, 0), (0, LANES - SSD_HEADS)))], axis=2).astype(BF16)


def _pad_lanes(v, n=LANES):
    return jnp.pad(v, (0, n - v.shape[0]))[None, :]


def kernel(x, mem, w_in, b_gate, lam_q1, lam_k1, lam_q2, lam_k2, attn_subln_g, rel_bias_table,
           w_pool_group, pool_scale, ssd_conv_w, ssd_conv_b, ssd_dt_bias, ssd_A_log, ssd_D,
           ssd_norm_g, w_branch, w_mix_out, g_mix, g_xattn, g_mem, w_xq, w_xkv, w_xo, g_moe,
           w_router_group, b_router_group, w_router_expert, b_router_expert,
           w_exp_gate, w_exp_up, w_exp_down, g_final):
    B, S, D = x.shape
    T = B * S
    depth = w_in.shape[0]
    nc = S // CHUNK
    lam_inits = tuple(0.8 - 0.6 * math.exp(-0.3 * l) for l in range(depth))
    bias, lam = _setup(rel_bias_table, lam_q1, lam_k1, lam_q2, lam_k2, lam_inits)

    xf = x.reshape(T, D)
    w_proj = _prep_inproj_weight(w_in)
    w_branch_b, w_mix_b = w_branch.astype(BF16), w_mix_out.astype(BF16)
    w_xq_b, w_xkv_b, w_xo_b = w_xq.astype(BF16), w_xkv.astype(BF16), w_xo.astype(BF16)
    w_r = jnp.concatenate(
        [jnp.pad(w_router_group, ((0, 0), (0, 0), (0, LANES - N_EGROUPS))),
         jnp.pad(w_router_expert, ((0, 0), (0, 0), (0, LANES - N_EXPERTS)))], axis=2).astype(BF16)
    moe = None
    for l in range(depth):
        outs = _inproj(
            xf, g_mix[l][None, :], w_proj, b_gate[l][None, :],
            jnp.repeat(ssd_dt_bias[l], SSD_HEAD_DIM)[None, :], _pad_lanes(ssd_dt_bias[l]),
            _pad_lanes(ssd_A_log[l]), l, moe)
        if moe is not None:
            xf, outs = outs[0], outs[1:]
        qkv, pool_u, z, xbc, gates, dtx, cs8 = outs
        o_a = _attention(qkv, bias, lam[l][None, :], attn_subln_g[l][None, :],
                         1.0 - lam_inits[l], B, S)
        o_b = _pool(pool_u, w_pool_group[l].astype(BF16), pool_scale[l][None, :], B, S)
        r = cs8[:, :SSD_HEADS].reshape(B, nc, CHUNK, SSD_HEADS).transpose(0, 1, 3, 2)
        r = r.reshape(B, nc, SSD_WIDTH)
        o_c = _ssd(xbc, z, dtx, r, ssd_conv_w[l], ssd_conv_b[l][None, :],
                   jnp.repeat(ssd_A_log[l], SSD_HEAD_DIM)[None, :],
                   jnp.repeat(ssd_D[l], SSD_HEAD_DIM)[None, :], ssd_norm_g[l][None, :], B, S)
        kv = _memkv(mem, g_mem[l][None, :], w_xkv_b, l)
        b_r = jnp.concatenate([_pad_lanes(b_router_group[l]), _pad_lanes(b_router_expert[l])],
                              axis=1)
        x2, hp, route, cnt = _merge(
            xf, o_a.reshape(T, A_WIDTH), o_b.reshape(T, POOL_WIDTH), o_c.reshape(T, SSD_WIDTH),
            gates, w_branch_b, w_mix_b, g_xattn[l][None, :], w_xq_b, kv, w_xo_b,
            g_moe[l][None, :], w_r, b_r, S, l)
        slots, tile_ea, tile_eb, tile_src, n_used, pad_tile = _plan(route, cnt, T)
        hs = _scatter(slots, pad_tile, n_used, hp, T + N_CLASSES * TME)
        ys = _experts(tile_ea, tile_eb, tile_src, n_used, hs, w_exp_gate, w_exp_up, w_exp_down, l)
        xf, moe = x2, (slots, ys)
    return _final(moe[0], xf, moe[1], g_final[None, :]).reshape(B, S, D)
```
